```python
import math
import jax, jax.numpy as jnp
from jax import lax
import numpy as np

D_MODEL = 2048
BATCH = 2
SEQ = 16384
DEPTH = 1

N_DIFF_HEADS = 8
DIFF_HEAD_DIM = 64
DIFF_V_DIM = 2 * DIFF_HEAD_DIM
QK_WIDTH = N_DIFF_HEADS * 2 * DIFF_HEAD_DIM
DIFF_WIDTH = N_DIFF_HEADS * DIFF_V_DIM
N_SG_GROUPS = 8
SG_GROUP_DIM = 128
SG_WIDTH = N_SG_GROUPS * SG_GROUP_DIM
SG_CHUNK = 128
MIX_WIDTH = DIFF_WIDTH + SG_WIDTH
IN_WIDTH = 2 * QK_WIDTH + DIFF_WIDTH + 2 * SG_WIDTH
ROPE_THETA = 500000.0
ROT_DIM = DIFF_HEAD_DIM // 4
Q_BLOCK = 128
N_EXPERTS = 256
TOP_K = 8
N_EXPERT_GROUPS = 8
TOPK_GROUPS = 4
EXPERT_FF = 512
SHARED_FF = 512
ROUTED_SCALE = 2.5
MOE_BLOCK = 128
LN_EPS = 1e-5
DEEPNORM_ALPHA = (2 * DEPTH) ** 0.25
DEEPNORM_BETA = (8 * DEPTH) ** -0.25

kernel_name = "hymba_diffattn_sgu_deepseek_moe_deepnorm"


def layer_norm(x, g, b):
    xf = x.astype(jnp.float32)
    mu = jnp.mean(xf, axis=-1, keepdims=True)
    xc = xf - mu
    var = jnp.mean(xc * xc, axis=-1, keepdims=True)
    y = xc * lax.rsqrt(var + LN_EPS) * g.astype(jnp.float32) + b.astype(jnp.float32)
    return y.astype(x.dtype)


def rms_norm(x, g):
    xf = x.astype(jnp.float32)
    y = xf * lax.rsqrt(jnp.mean(xf * xf, axis=-1, keepdims=True) + LN_EPS) * g.astype(jnp.float32)
    return y.astype(x.dtype)


def partial_rope(x, cos, sin):
    xf = x.astype(jnp.float32)
    half = ROT_DIM // 2
    x1 = xf[..., :half]
    x2 = xf[..., half:ROT_DIM]
    out = jnp.concatenate([x1 * cos - x2 * sin, x2 * cos + x1 * sin, xf[..., ROT_DIM:]], axis=-1)
    return out.astype(x.dtype)


def diff_attention(q, k, v, lam):
    B, S = q.shape[0], q.shape[1]
    nq = S // Q_BLOCK
    q_blocks = jnp.moveaxis(q.reshape(B, nq, Q_BLOCK, N_DIFF_HEADS, 2, DIFF_HEAD_DIM), 1, 0)
    key_pos = jnp.arange(S)
    scale = DIFF_HEAD_DIM ** -0.5

    def one_block(args):
        q_blk, i = args
        q_pos = i * Q_BLOCK + jnp.arange(Q_BLOCK)
        s = jnp.einsum('bqhcd,bkhcd->bchqk', q_blk, k).astype(jnp.float32) * scale
        mask = key_pos[None, :] <= q_pos[:, None]
        s = jnp.where(mask, s, -jnp.inf)
        p = jax.nn.softmax(s, axis=-1)
        a = p[:, 0] - lam * p[:, 1]
        return jnp.einsum('bhqk,bkhe->bqhe', a.astype(v.dtype), v)

    o = lax.map(one_block, (q_blocks, jnp.arange(nq)))
    return jnp.moveaxis(o, 0, 1).reshape(B, S, N_DIFF_HEADS, DIFF_V_DIM)


def spatial_gating(u, v, ln_g, ln_b, w_s, b_s):
    B, S = u.shape[0], u.shape[1]
    nc = S // SG_CHUNK
    vn = layer_norm(v.reshape(B, S, N_SG_GROUPS, SG_GROUP_DIM), ln_g, ln_b)
    vc = vn.reshape(B, nc, SG_CHUNK, N_SG_GROUPS, SG_GROUP_DIM)
    causal = jnp.tril(jnp.ones((SG_CHUNK, SG_CHUNK), dtype=w_s.dtype))
    mixed = jnp.einsum('gts,bnsgc->bntgc', w_s * causal, vc) + b_s.T[None, None, :, :, None]
    return u * mixed.reshape(B, S, SG_WIDTH)


def swiglu(x, w_gate, w_up, w_down):
    return (jax.nn.silu(x @ w_gate) * (x @ w_up)) @ w_down


def moe_ffn(x, w_router, router_bias, w_gate_e, w_up_e, w_down_e, w_gate_s, w_up_s, w_down_s):
    B, S, D = x.shape
    T = B * S
    TK = T * TOP_K
    x2 = x.reshape(T, D)
    scores = jax.nn.sigmoid((x2.astype(jnp.float32) @ w_router.astype(jnp.float32)))
    choice = scores + router_bias.astype(jnp.float32)
    per_group = N_EXPERTS // N_EXPERT_GROUPS
    group_scores = lax.top_k(choice.reshape(T, N_EXPERT_GROUPS, per_group), 2)[0].sum(-1)
    _, gidx = lax.top_k(group_scores, TOPK_GROUPS)
    gmask = jax.nn.one_hot(gidx, N_EXPERT_GROUPS, dtype=jnp.float32).sum(1) > 0
    emask = jnp.repeat(gmask, per_group, axis=1)
    _, tidx = lax.top_k(jnp.where(emask, choice, -jnp.inf), TOP_K)
    w = jnp.take_along_axis(scores, tidx, axis=1)
    w = w / jnp.sum(w, axis=-1, keepdims=True) * ROUTED_SCALE

    expert_flat = tidx.reshape(TK)
    token_flat = jnp.arange(TK, dtype=jnp.int32) // TOP_K
    w_flat = w.reshape(TK)
    order = jnp.argsort(expert_flat, stable=True)
    sorted_e = expert_flat[order]
    counts = jnp.bincount(expert_flat, length=N_EXPERTS)
    padded = ((counts + MOE_BLOCK - 1) // MOE_BLOCK) * MOE_BLOCK
    pad_end = jnp.cumsum(padded)
    pad_off = pad_end - padded
    off = jnp.cumsum(counts) - counts
    dest = pad_off[sorted_e] + (jnp.arange(TK) - off[sorted_e])
    P = TK + N_EXPERTS * MOE_BLOCK
    NB = P // MOE_BLOCK
    tok_buf = jnp.full((P,), T, dtype=jnp.int32).at[dest].set(token_flat[order])
    wt_buf = jnp.zeros((P,), jnp.float32).at[dest].set(w_flat[order])
    block_expert = jnp.clip(jnp.searchsorted(pad_end, jnp.arange(NB) * MOE_BLOCK, side='right'), 0, N_EXPERTS - 1)
    x_pad = jnp.concatenate([x2, jnp.zeros((1, D), x2.dtype)], axis=0)

    def step(acc, blk):
        tok, wt, e = blk
        xb = x_pad[tok]
        y = swiglu(xb, w_gate_e[e], w_up_e[e], w_down_e[e]).astype(jnp.float32) * wt[:, None]
        return acc.at[tok].add(y), None

    acc, _ = lax.scan(step, jnp.zeros((T + 1, D), jnp.float32),
                      (tok_buf.reshape(NB, MOE_BLOCK), wt_buf.reshape(NB, MOE_BLOCK), block_expert))
    routed = acc[:T].astype(x.dtype)
    shared = swiglu(x2, w_gate_s, w_up_s, w_down_s)
    return (routed + shared).reshape(B, S, D)


def setup_inputs(seed: int = 0) -> dict:
    key = jax.random.key(seed)
    ks = jax.random.split(key, 24)
    f32 = jnp.float32
    nrm = lambda k, shape, s: jax.random.normal(k, shape, f32) * s
    d_in = D_MODEL ** -0.5
    w_qk = nrm(ks[0], (DEPTH, D_MODEL, 2 * QK_WIDTH), d_in)
    w_v = nrm(ks[1], (DEPTH, D_MODEL, DIFF_WIDTH), d_in) * DEEPNORM_BETA
    w_sg = nrm(ks[2], (DEPTH, D_MODEL, 2 * SG_WIDTH), d_in)
    return {
        "x": jax.random.normal(ks[3], (BATCH, SEQ, D_MODEL), f32),
        "positions": jnp.broadcast_to(jnp.arange(SEQ, dtype=jnp.int32), (BATCH, SEQ)),
        "w_in": jnp.concatenate([w_qk, w_v, w_sg], axis=-1),
        "lam_q1": nrm(ks[4], (DEPTH, DIFF_HEAD_DIM), 0.1),
        "lam_k1": nrm(ks[5], (DEPTH, DIFF_HEAD_DIM), 0.1),
        "lam_q2": nrm(ks[6], (DEPTH, DIFF_HEAD_DIM), 0.1),
        "lam_k2": nrm(ks[7], (DEPTH, DIFF_HEAD_DIM), 0.1),
        "subln_g": 1.0 + nrm(ks[8], (DEPTH, DIFF_V_DIM), 0.02),
        "sgu_ln_g": 1.0 + nrm(ks[9], (DEPTH, N_SG_GROUPS, SG_GROUP_DIM), 0.02),
        "sgu_ln_b": nrm(ks[10], (DEPTH, N_SG_GROUPS, SG_GROUP_DIM), 0.02),
        "w_spatial": nrm(ks[11], (DEPTH, N_SG_GROUPS, SG_CHUNK, SG_CHUNK), SG_CHUNK ** -0.5),
        "b_spatial": 1.0 + nrm(ks[12], (DEPTH, N_SG_GROUPS, SG_CHUNK), 0.02),
        "w_out": nrm(ks[13], (DEPTH, MIX_WIDTH, D_MODEL), MIX_WIDTH ** -0.5 * DEEPNORM_BETA),
        "ln1_g": 1.0 + nrm(ks[14], (DEPTH, D_MODEL), 0.02),
        "ln1_b": nrm(ks[15], (DEPTH, D_MODEL), 0.02),
        "w_router": nrm(ks[16], (DEPTH, D_MODEL, N_EXPERTS), d_in),
        "router_bias": nrm(ks[17], (DEPTH, N_EXPERTS), 0.01),
        "w_gate_exp": nrm(ks[18], (DEPTH, N_EXPERTS, D_MODEL, EXPERT_FF), d_in),
        "w_up_exp": nrm(ks[19], (DEPTH, N_EXPERTS, D_MODEL, EXPERT_FF), d_in),
        "w_down_exp": nrm(ks[20], (DEPTH, N_EXPERTS, EXPERT_FF, D_MODEL), EXPERT_FF ** -0.5 * DEEPNORM_BETA),
        "w_gate_sh": nrm(ks[21], (DEPTH, D_MODEL, SHARED_FF), d_in),
        "w_up_sh": nrm(ks[22], (DEPTH, D_MODEL, SHARED_FF), d_in),
        "w_down_sh": nrm(ks[23], (DEPTH, SHARED_FF, D_MODEL), SHARED_FF ** -0.5 * DEEPNORM_BETA),
        "ln2_g": 1.0 + nrm(jax.random.fold_in(key, 101), (DEPTH, D_MODEL), 0.02),
        "ln2_b": nrm(jax.random.fold_in(key, 102), (DEPTH, D_MODEL), 0.02),
    }


def reference(x, positions, w_in, lam_q1, lam_k1, lam_q2, lam_k2, subln_g, sgu_ln_g, sgu_ln_b,
              w_spatial, b_spatial, w_out, ln1_g, ln1_b, w_router, router_bias, w_gate_exp,
              w_up_exp, w_down_exp, w_gate_sh, w_up_sh, w_down_sh, ln2_g, ln2_b):
    B, S, _ = x.shape
    inv_freq = ROPE_THETA ** (-jnp.arange(0, ROT_DIM, 2, dtype=jnp.float32) / ROT_DIM)
    ang = positions.astype(jnp.float32)[..., None] * inv_freq
    cos = jnp.cos(ang)[:, :, None, None, :]
    sin = jnp.sin(ang)[:, :, None, None, :]
    for l in range(DEPTH):
        lambda_init = 0.8 - 0.6 * math.exp(-0.3 * l)
        proj = x @ w_in[l]
        q = proj[..., :QK_WIDTH].reshape(B, S, N_DIFF_HEADS, 2, DIFF_HEAD_DIM)
        k = proj[..., QK_WIDTH:2 * QK_WIDTH].reshape(B, S, N_DIFF_HEADS, 2, DIFF_HEAD_DIM)
        v = proj[..., 2 * QK_WIDTH:2 * QK_WIDTH + DIFF_WIDTH].reshape(B, S, N_DIFF_HEADS, DIFF_V_DIM)
        q = partial_rope(q, cos, sin)
        k = partial_rope(k, cos, sin)
        lam = (jnp.exp(jnp.sum(lam_q1[l].astype(jnp.float32) * lam_k1[l].astype(jnp.float32)))
               - jnp.exp(jnp.sum(lam_q2[l].astype(jnp.float32) * lam_k2[l].astype(jnp.float32)))
               + lambda_init)
        attn = diff_attention(q, k, v, lam)
        attn = (rms_norm(attn, subln_g[l]) * (1.0 - lambda_init)).reshape(B, S, DIFF_WIDTH)
        z = jax.nn.gelu(proj[..., 2 * QK_WIDTH + DIFF_WIDTH:], approximate=False)
        sg = spatial_gating(z[..., :SG_WIDTH], z[..., SG_WIDTH:], sgu_ln_g[l], sgu_ln_b[l],
                            w_spatial[l], b_spatial[l])
        mix = jnp.concatenate([attn, sg], axis=-1) @ w_out[l]
        x = layer_norm(DEEPNORM_ALPHA * x + mix, ln1_g[l], ln1_b[l])
        ffn = moe_ffn(x, w_router[l], router_bias[l], w_gate_exp[l], w_up_exp[l], w_down_exp[l],
                      w_gate_sh[l], w_up_sh[l], w_down_sh[l])
        x = layer_norm(DEEPNORM_ALPHA * x + ffn, ln2_g[l], ln2_b[l])
    return x
```

```python
import functools
import math

import jax
import jax.numpy as jnp
from jax import lax
from jax.experimental import pallas as pl
from jax.experimental.pallas import tpu as pltpu

F32 = jnp.float32
BF16 = jnp.bfloat16
U32 = jnp.uint32
I32 = jnp.int32

N_HEADS = 8
HEAD_DIM = 64
V_DIM = 2 * HEAD_DIM
N_SG_GROUPS = 8
SG_DIM = 128
SG_CHUNK = 128
ROPE_THETA = 500000.0
ROT_DIM = HEAD_DIM // 4
TOP_K = 8
N_EXPERT_GROUPS = 8
TOPK_GROUPS = 4
ROUTED_SCALE = 2.5
LN_EPS = 1e-5
DEPTH = 1
ALPHA = (2 * DEPTH) ** 0.25
LAMBDA_INIT = 0.8 - 0.6 * math.exp(-0.3 * 0)
QK_SCALE = HEAD_DIM ** -0.5

LANES = 128
SEG = 1024
VMEM_LIMIT = 56 * 1024 * 1024


def _cparams(sem, vmem=VMEM_LIMIT):
    return pltpu.CompilerParams(dimension_semantics=sem, vmem_limit_bytes=vmem)


def _pick(n, prefs):
    for p in prefs:
        if n % p == 0:
            return p
    return n


def _inproj_kernel(x_ref, w_ref, c_ref, sa_ref, sb_ref, o_ref, xb_ref):
    j = pl.program_id(1)

    @pl.when(j == 0)
    def _():
        xb_ref[...] = x_ref[...].astype(BF16)

    acc = jnp.dot(xb_ref[...], w_ref[...], preferred_element_type=F32)

    @pl.when(j < 2)
    def _():
        scale = jnp.where(j == 0, QK_SCALE, 1.0).astype(F32)
        c = c_ref[...] * scale
        sa = sa_ref[...] * scale
        sb = sb_ref[...] * scale
        for blk in range(SEG // LANES):
            a = acc[:, blk * LANES:(blk + 1) * LANES]
            r = a * c + pltpu.roll(a, 8, 1) * sa + pltpu.roll(a, LANES - 8, 1) * sb
            o_ref[:, blk * LANES:(blk + 1) * LANES] = r.astype(BF16)

    @pl.when(j == 2)
    def _():
        o_ref[...] = acc.astype(BF16)

    @pl.when(j > 2)
    def _():
        g = 0.5 * acc * (1.0 + lax.erf(acc * (2.0 ** -0.5)))
        o_ref[...] = g.astype(BF16)


def _inproj(x2, w_b, rc, rsa, rsb):
    T, D = x2.shape
    n_seg = w_b.shape[1] // SEG
    tm = _pick(T, (512, 256, 128))
    return pl.pallas_call(
        _inproj_kernel,
        grid=(T // tm, n_seg),
        in_specs=[
            pl.BlockSpec((tm, D), lambda i, j: (i, 0)),
            pl.BlockSpec((D, SEG), lambda i, j: (0, j)),
            pl.BlockSpec((tm, LANES), lambda i, j: (i, 0)),
            pl.BlockSpec((tm, LANES), lambda i, j: (i, 0)),
            pl.BlockSpec((tm, LANES), lambda i, j: (i, 0)),
        ],
        out_specs=pl.BlockSpec((tm, SEG), lambda i, j: (i, j)),
        out_shape=jax.ShapeDtypeStruct((T, n_seg * SEG), BF16),
        scratch_shapes=[pltpu.VMEM((tm, D), BF16)],
        compiler_params=_cparams(("parallel", "arbitrary")),
        name="inproj",
    )(x2, w_b, rc, rsa, rsb)


def _attn_kernel(lam_ref, q_ref, k_ref, v_ref, g_ref, o_ref,
                 acc_a, acc_b, m_a, l_a, m_b, l_b, *, tq):
    qi = pl.program_id(2)
    q = q_ref[...]
    lane = lax.broadcasted_iota(I32, q.shape, 1)
    zero = jnp.zeros_like(q)
    q_a = jnp.where(lane < HEAD_DIM, q, zero)
    q_b = jnp.where(lane >= HEAD_DIM, q, zero)

    for r in (acc_a, acc_b, l_a, l_b):
        r[...] = jnp.zeros(r.shape, F32)
    for r in (m_a, m_b):
        r[...] = jnp.full(r.shape, -jnp.inf, F32)

    def block(ki, masked):
        start = pl.multiple_of(ki * tq, tq)
        k = k_ref[pl.ds(start, tq), :]
        v = v_ref[pl.ds(start, tq), :]
        if masked:
            row = lax.broadcasted_iota(I32, (tq, tq), 0)
            col = lax.broadcasted_iota(I32, (tq, tq), 1)
            keep = col <= row
        for qq, acc, m, l in ((q_a, acc_a, m_a, l_a), (q_b, acc_b, m_b, l_b)):
            s = lax.dot_general(qq, k, (((1,), (1,)), ((), ())), preferred_element_type=F32)
            if masked:
                s = jnp.where(keep, s, -jnp.inf)
            m_prev = m[...]
            m_new = jnp.maximum(m_prev, jnp.max(s, axis=1, keepdims=True))
            alpha = jnp.exp(m_prev - m_new)
            p = jnp.exp(s - m_new)
            l[...] = alpha * l[...] + jnp.sum(p, axis=1, keepdims=True)
            acc[...] = alpha * acc[...] + jnp.dot(p.astype(BF16), v, preferred_element_type=F32)
            m[...] = m_new

    def body(ki, carry):
        block(ki, False)
        return carry

    lax.fori_loop(0, qi, body, 0)
    block(qi, True)

    lam = lam_ref[0, 0]
    o = acc_a[...] / l_a[...] - lam * (acc_b[...] / l_b[...])
    ms = jnp.mean(o * o, axis=1, keepdims=True)
    o = o * lax.rsqrt(ms + LN_EPS) * g_ref[...] * (1.0 - LAMBDA_INIT)
    o_ref[...] = o.astype(BF16)


def _attention(proj, lam, subln_g, B, S):
    T = B * S
    tq = _pick(S, (512, 256, 128))
    nq = S // tq
    kern = functools.partial(_attn_kernel, tq=tq)
    return pl.pallas_call(
        kern,
        grid=(B, N_HEADS, nq),
        in_specs=[
            pl.BlockSpec(memory_space=pltpu.SMEM),
            pl.BlockSpec((tq, V_DIM), lambda b, h, i: (b * nq + i, h)),
            pl.BlockSpec((S, V_DIM), lambda b, h, i: (b, N_HEADS + h)),
            pl.BlockSpec((S, V_DIM), lambda b, h, i: (b, 2 * N_HEADS + h)),
            pl.BlockSpec((1, V_DIM), lambda b, h, i: (0, 0)),
        ],
        out_specs=pl.BlockSpec((tq, V_DIM), lambda b, h, i: (b * nq + i, h)),
        out_shape=jax.ShapeDtypeStruct((T, N_HEADS * V_DIM), BF16),
        scratch_shapes=[
            pltpu.VMEM((tq, V_DIM), F32), pltpu.VMEM((tq, V_DIM), F32),
            pltpu.VMEM((tq, 1), F32), pltpu.VMEM((tq, 1), F32),
            pltpu.VMEM((tq, 1), F32), pltpu.VMEM((tq, 1), F32),
        ],
        compiler_params=_cparams(("parallel", "parallel", "arbitrary")),
        name="diff_attention",
    )(lam, proj, proj, proj, subln_g)


def _sgu_kernel(u_ref, v_ref, g_ref, b_ref, w_ref, bs_ref, o_ref, *, tm):
    g = pl.program_id(1)
    ln_g = g_ref[pl.ds(g, 1), :]
    ln_b = b_ref[pl.ds(g, 1), :]
    row = lax.broadcasted_iota(I32, (SG_CHUNK, SG_CHUNK), 0)
    col = lax.broadcasted_iota(I32, (SG_CHUNK, SG_CHUNK), 1)
    w = jnp.where(col <= row, w_ref[0], 0.0).astype(BF16)
    bs = bs_ref[0]
    for c in range(tm // SG_CHUNK):
        sl = slice(c * SG_CHUNK, (c + 1) * SG_CHUNK)
        v = v_ref[sl, :].astype(F32)
        mu = jnp.mean(v, axis=1, keepdims=True)
        vc = v - mu
        var = jnp.mean(vc * vc, axis=1, keepdims=True)
        vn = vc * lax.rsqrt(var + LN_EPS) * ln_g + ln_b
        mixed = jnp.dot(w, vn.astype(BF16), preferred_element_type=F32) + bs
        o_ref[sl, :] = (u_ref[sl, :].astype(F32) * mixed).astype(BF16)


def _sgu(proj, ln_g, ln_b, w_s, bs_full):
    T = proj.shape[0]
    tm = _pick(T, (1024, 512, 256, 128))
    u_blk = 3 * SEG // SG_DIM
    v_blk = 4 * SEG // SG_DIM
    kern = functools.partial(_sgu_kernel, tm=tm)
    return pl.pallas_call(
        kern,
        grid=(T // tm, N_SG_GROUPS),
        in_specs=[
            pl.BlockSpec((tm, SG_DIM), lambda i, g: (i, u_blk + g)),
            pl.BlockSpec((tm, SG_DIM), lambda i, g: (i, v_blk + g)),
            pl.BlockSpec((N_SG_GROUPS, SG_DIM), lambda i, g: (0, 0)),
            pl.BlockSpec((N_SG_GROUPS, SG_DIM), lambda i, g: (0, 0)),
            pl.BlockSpec((1, SG_CHUNK, SG_CHUNK), lambda i, g: (g, 0, 0)),
            pl.BlockSpec((1, SG_CHUNK, SG_DIM), lambda i, g: (g, 0, 0)),
        ],
        out_specs=pl.BlockSpec((tm, SG_DIM), lambda i, g: (i, g)),
        out_shape=jax.ShapeDtypeStruct((T, N_SG_GROUPS * SG_DIM), BF16),
        compiler_params=_cparams(("parallel", "arbitrary")),
        name="spatial_gating",
    )(proj, proj, ln_g, ln_b, w_s, bs_full)


def _pack_pairs(y):
    half = y.shape[1] // 2
    lo = lax.bitcast_convert_type(y[:, :half].astype(BF16).astype(F32), U32)
    hi = lax.bitcast_convert_type(y[:, half:].astype(BF16).astype(F32), U32)
    return (lo >> 16) | (hi & jnp.uint32(0xFFFF0000))


def _unpack_pairs(p):
    lo = lax.bitcast_convert_type(p << 16, F32)
    hi = lax.bitcast_convert_type(p & jnp.uint32(0xFFFF0000), F32)
    return lo, hi


def _layer_norm(y, g, b):
    mu = jnp.mean(y, axis=1, keepdims=True)
    yc = y - mu
    var = jnp.mean(yc * yc, axis=1, keepdims=True)
    return yc * lax.rsqrt(var + LN_EPS) * g + b


def _outproj_kernel(a_ref, s_ref, x_ref, w_ref, g_ref, b_ref, x1_ref, xp_ref):
    half = w_ref.shape[0] // 2
    mix = jnp.dot(a_ref[...], w_ref[:half, :], preferred_element_type=F32)
    mix = mix + jnp.dot(s_ref[...], w_ref[half:, :], preferred_element_type=F32)
    y = ALPHA * x_ref[...] + mix
    x1 = _layer_norm(y, g_ref[...], b_ref[...])
    x1_ref[...] = x1
    xp_ref[...] = _pack_pairs(x1)


def _outproj(attn, sg, x2, w_b, g, b):
    T, D = x2.shape
    tm = _pick(T, (256, 128))
    wa = attn.shape[1]
    ws = sg.shape[1]
    return pl.pallas_call(
        _outproj_kernel,
        grid=(T // tm,),
        in_specs=[
            pl.BlockSpec((tm, wa), lambda i: (i, 0)),
            pl.BlockSpec((tm, ws), lambda i: (i, 0)),
            pl.BlockSpec((tm, D), lambda i: (i, 0)),
            pl.BlockSpec((wa + ws, D), lambda i: (0, 0)),
            pl.BlockSpec((1, D), lambda i: (0, 0)),
            pl.BlockSpec((1, D), lambda i: (0, 0)),
        ],
        out_specs=[
            pl.BlockSpec((tm, D), lambda i: (i, 0)),
            pl.BlockSpec((tm, D // 2), lambda i: (i, 0)),
        ],
        out_shape=[
            jax.ShapeDtypeStruct((T, D), F32),
            jax.ShapeDtypeStruct((T, D // 2), U32),
        ],
        compiler_params=_cparams(("parallel",)),
        name="outproj_ln1",
    )(attn, sg, x2, w_b, g, b)


def _router_kernel(x_ref, wh_ref, wl_ref, bias_ref, idx_ref, wt_ref, rank_ref, cnt_ref,
                   carry_ref, *, tm, n_exp):
    step = pl.program_id(0)

    @pl.when(step == 0)
    def _():
        carry_ref[...] = jnp.zeros(carry_ref.shape, F32)

    x = x_ref[...]
    xh = x.astype(BF16)
    xl = (x - xh.astype(F32)).astype(BF16)
    dn = (((1,), (1,)), ((), ()))
    wh = wh_ref[...]
    logits = (lax.dot_general(wh, xh, dn, preferred_element_type=F32)
              + lax.dot_general(wh, xl, dn, preferred_element_type=F32)
              + lax.dot_general(wl_ref[...], xh, dn, preferred_element_type=F32))
    scores = jax.nn.sigmoid(logits)
    choice = scores + bias_ref[...]

    per = n_exp // N_EXPERT_GROUPS
    neg = jnp.float32(-jnp.inf)
    sub = lax.broadcasted_iota(I32, (per, tm), 0)
    gscore = []
    for g in range(N_EXPERT_GROUPS):
        cg = choice[g * per:(g + 1) * per, :]
        m1 = jnp.max(cg, axis=0, keepdims=True)
        i1 = jnp.min(jnp.where(cg == m1, sub, per), axis=0, keepdims=True)
        m2 = jnp.max(jnp.where(sub == i1, neg, cg), axis=0, keepdims=True)
        gscore.append(m1 + m2)
    masked_parts = []
    for g in range(N_EXPERT_GROUPS):
        beat = jnp.zeros((1, tm), I32)
        for h in range(N_EXPERT_GROUPS):
            if h == g:
                continue
            wins = (gscore[h] > gscore[g]) | ((gscore[h] == gscore[g]) & (h < g))
            beat = beat + wins.astype(I32)
        keep = beat < TOPK_GROUPS
        cg = choice[g * per:(g + 1) * per, :]
        masked_parts.append(jnp.where(keep, cg, neg))
    masked = jnp.concatenate(masked_parts, axis=0)

    eidx = lax.broadcasted_iota(I32, (n_exp, tm), 0)
    hot = jnp.zeros((n_exp, tm), F32)
    sel_idx, sel_w, sel_hot = [], [], []
    for _ in range(TOP_K):
        mk = jnp.max(masked, axis=0, keepdims=True)
        ik = jnp.min(jnp.where(masked == mk, eidx, n_exp), axis=0, keepdims=True)
        one = eidx == ik
        sel_idx.append(ik)
        sel_w.append(jnp.sum(jnp.where(one, scores, 0.0), axis=0, keepdims=True))
        sel_hot.append(one)
        hot = hot + one.astype(F32)
        masked = jnp.where(one, neg, masked)

    wsum = sel_w[0]
    for k in range(1, TOP_K):
        wsum = wsum + sel_w[k]

    r = lax.broadcasted_iota(I32, (tm, tm), 0)
    c = lax.broadcasted_iota(I32, (tm, tm), 1)
    upper = (r < c).astype(BF16)
    pref = jnp.dot(hot.astype(BF16), upper, preferred_element_type=F32) + carry_ref[...]
    for k in range(TOP_K):
        idx_ref[k:k + 1, :] = sel_idx[k]
        wt_ref[k:k + 1, :] = sel_w[k] / wsum * ROUTED_SCALE
        rk = jnp.sum(jnp.where(sel_hot[k], pref, 0.0), axis=0, keepdims=True)
        rank_ref[k:k + 1, :] = rk.astype(I32)
    carry_ref[...] = carry_ref[...] + jnp.sum(hot, axis=1, keepdims=True)
    cnt_ref[...] = carry_ref[...]


def _router(x1, wh, wl, bias_col):
    T, D = x1.shape
    n_exp = wh.shape[0]
    tm = _pick(T, (512, 256, 128))
    kern = functools.partial(_router_kernel, tm=tm, n_exp=n_exp)
    return pl.pallas_call(
        kern,
        grid=(T // tm,),
        in_specs=[
            pl.BlockSpec((tm, D), lambda i: (i, 0)),
            pl.BlockSpec((n_exp, D), lambda i: (0, 0)),
            pl.BlockSpec((n_exp, D), lambda i: (0, 0)),
            pl.BlockSpec((n_exp, 1), lambda i: (0, 0)),
        ],
        out_specs=[
            pl.BlockSpec((TOP_K, tm), lambda i: (0, i)),
            pl.BlockSpec((TOP_K, tm), lambda i: (0, i)),
            pl.BlockSpec((TOP_K, tm), lambda i: (0, i)),
            pl.BlockSpec((n_exp, 1), lambda i: (0, 0)),
        ],
        out_shape=[
            jax.ShapeDtypeStruct((TOP_K, T), I32),
            jax.ShapeDtypeStruct((TOP_K, T), F32),
            jax.ShapeDtypeStruct((TOP_K, T), I32),
            jax.ShapeDtypeStruct((n_exp, 1), F32),
        ],
        scratch_shapes=[pltpu.VMEM((n_exp, 1), F32)],
        compiler_params=_cparams(("arbitrary",)),
        name="router_topk",
    )(x1, wh, wl, bias_col)


def _dispatch_kernel(cnt_ref, off_ref, dest_ref, x_ref, xs_ref, zero_ref, sem, *,
                     tm, blk, n_exp, exp_per_step):
    step = pl.program_id(0)
    n_rows = tm * TOP_K

    def row_copy(t, d):
        return pltpu.make_async_copy(x_ref.at[pl.ds(t, 1), :], xs_ref.at[pl.ds(d, 1), :], sem)

    def issue(i, carry):
        row_copy(i // TOP_K, dest_ref[i]).start()
        return carry

    lax.fori_loop(0, n_rows, issue, 0)

    zero_ref[...] = jnp.zeros(zero_ref.shape, U32)

    def zero_copy(d):
        return pltpu.make_async_copy(zero_ref, xs_ref.at[pl.ds(d, 1), :], sem)

    n_pad_total = jnp.int32(0)
    for j in range(exp_per_step):
        e = jnp.minimum(step * exp_per_step + j, n_exp - 1)
        valid = (step * exp_per_step + j) < n_exp
        cnt = cnt_ref[e]
        n_pad = jnp.where(valid, (blk - cnt % blk) % blk, 0)
        base = off_ref[e] + cnt

        def zissue(i, carry, base=base):
            zero_copy(base + i).start()
            return carry

        lax.fori_loop(0, n_pad, zissue, 0)
        n_pad_total = n_pad_total + n_pad

    def wait(i, carry):
        row_copy(0, 0).wait()
        return carry

    lax.fori_loop(0, n_rows + n_pad_total, wait, 0)


def _dispatch(counts, pad_off, dest_flat, x1p, n_rows_out, blk):
    T, W = x1p.shape
    n_exp = counts.shape[0]
    tm = _pick(T, (256, 128))
    n_steps = T // tm
    exp_per_step = -(-n_exp // n_steps)
    kern = functools.partial(_dispatch_kernel, tm=tm, blk=blk, n_exp=n_exp,
                             exp_per_step=exp_per_step)
    grid_spec = pltpu.PrefetchScalarGridSpec(
        num_scalar_prefetch=2,
        grid=(n_steps,),
        in_specs=[
            pl.BlockSpec((tm * TOP_K,), lambda i, c, o: (i,), memory_space=pltpu.SMEM),
            pl.BlockSpec((tm, W), lambda i, c, o: (i, 0)),
        ],
        out_specs=pl.BlockSpec(memory_space=pl.ANY),
        scratch_shapes=[pltpu.VMEM((1, W), U32), pltpu.SemaphoreType.DMA(())],
    )
    return pl.pallas_call(
        kern,
        grid_spec=grid_spec,
        out_shape=jax.ShapeDtypeStruct((n_rows_out, W), U32),
        compiler_params=_cparams(("arbitrary",)),
        name="moe_dispatch",
    )(counts, pad_off, dest_flat, x1p)


def _gmm_kernel(be_ref, nb_ref, xs_ref, wg_ref, wu_ref, wd_ref, ys_ref, wg_b, wu_b, wd_b):
    b = pl.program_id(0)
    prev = be_ref[jnp.maximum(b - 1, 0)]
    fresh = (b == 0) | (be_ref[b] != prev)
    live = b < nb_ref[0]

    @pl.when(live & fresh)
    def _():
        wg_b[...] = wg_ref[0].astype(BF16)
        wu_b[...] = wu_ref[0].astype(BF16)
        wd_b[...] = wd_ref[0].astype(BF16)

    @pl.when(live)
    def _():
        lo, hi = _unpack_pairs(xs_ref[...])
        lo = lo.astype(BF16)
        hi = hi.astype(BF16)
        half = lo.shape[1]
        g = (jnp.dot(lo, wg_b[:half, :], preferred_element_type=F32)
             + jnp.dot(hi, wg_b[half:, :], preferred_element_type=F32))
        u = (jnp.dot(lo, wu_b[:half, :], preferred_element_type=F32)
             + jnp.dot(hi, wu_b[half:, :], preferred_element_type=F32))
        h = (g * jax.nn.sigmoid(g) * u).astype(BF16)
        y = jnp.dot(h, wd_b[...], preferred_element_type=F32)
        ys_ref[...] = _pack_pairs(y)


def _gmm(block_expert, nb_used, xs, wg, wu, wd, blk):
    P, W = xs.shape
    n_exp, D, FF = wg.shape
    nb = P // blk

    def row_map(b, be, nbu):
        return (jnp.minimum(b, nbu[0] - 1), 0)

    def w_map(b, be, nbu):
        return (be[b], 0, 0)

    grid_spec = pltpu.PrefetchScalarGridSpec(
        num_scalar_prefetch=2,
        grid=(nb,),
        in_specs=[
            pl.BlockSpec((blk, W), row_map),
            pl.BlockSpec((1, D, FF), w_map),
            pl.BlockSpec((1, D, FF), w_map),
            pl.BlockSpec((1, FF, D), w_map),
        ],
        out_specs=pl.BlockSpec((blk, W), row_map),
        scratch_shapes=[
            pltpu.VMEM((D, FF), BF16), pltpu.VMEM((D, FF), BF16), pltpu.VMEM((FF, D), BF16),
        ],
    )
    return pl.pallas_call(
        _gmm_kernel,
        grid_spec=grid_spec,
        out_shape=jax.ShapeDtypeStruct((P, W), U32),
        compiler_params=_cparams(("arbitrary",)),
        name="moe_grouped_swiglu",
    )(block_expert, nb_used, xs, wg, wu, wd)


def _combine_kernel(dest_ref, wt_ref, x_ref, ys_ref, wg_ref, wu_ref, wd_ref, g_ref, b_ref,
                    o_ref, buf, sem, *, tm):
    n_rows = tm * TOP_K

    def row_copy(i, d):
        return pltpu.make_async_copy(ys_ref.at[pl.ds(d, 1), :],
                                     buf.at[i % TOP_K, pl.ds(i // TOP_K, 1), :], sem)

    def issue(i, carry):
        row_copy(i, dest_ref[i]).start()
        return carry

    lax.fori_loop(0, n_rows, issue, 0)

    x = x_ref[...]
    xb = x.astype(BF16)
    g = jnp.dot(xb, wg_ref[...], preferred_element_type=F32)
    u = jnp.dot(xb, wu_ref[...], preferred_element_type=F32)
    h = (g * jax.nn.sigmoid(g) * u).astype(BF16)
    shared = jnp.dot(h, wd_ref[...], preferred_element_type=F32)

    def wait(i, carry):
        row_copy(0, 0).wait()
        return carry

    lax.fori_loop(0, n_rows, wait, 0)

    half = x.shape[1] // 2
    wt = wt_ref[...]
    r_lo = jnp.zeros((tm, half), F32)
    r_hi = jnp.zeros((tm, half), F32)
    for k in range(TOP_K):
        lo, hi = _unpack_pairs(buf[k])
        wk = wt[:, k:k + 1]
        r_lo = r_lo + lo * wk
        r_hi = r_hi + hi * wk
    routed = jnp.concatenate([r_lo, r_hi], axis=1)
    y = ALPHA * x + (routed + shared)
    o_ref[...] = _layer_norm(y, g_ref[...], b_ref[...])


def _combine(dest_flat, wt_tok, x1, ys, wg_s, wu_s, wd_s, g, b):
    T, D = x1.shape
    W = ys.shape[1]
    FF = wg_s.shape[1]
    tm = _pick(T, (128,))
    kern = functools.partial(_combine_kernel, tm=tm)
    return pl.pallas_call(
        kern,
        grid=(T // tm,),
        in_specs=[
            pl.BlockSpec((tm * TOP_K,), lambda i: (i,), memory_space=pltpu.SMEM),
            pl.BlockSpec((tm, TOP_K), lambda i: (i, 0)),
            pl.BlockSpec((tm, D), lambda i: (i, 0)),
            pl.BlockSpec(memory_space=pl.ANY),
            pl.BlockSpec((D, FF), lambda i: (0, 0)),
            pl.BlockSpec((D, FF), lambda i: (0, 0)),
            pl.BlockSpec((FF, D), lambda i: (0, 0)),
            pl.BlockSpec((1, D), lambda i: (0, 0)),
            pl.BlockSpec((1, D), lambda i: (0, 0)),
        ],
        out_specs=pl.BlockSpec((tm, D), lambda i: (i, 0)),
        out_shape=jax.ShapeDtypeStruct((T, D), F32),
        scratch_shapes=[pltpu.VMEM((TOP_K, tm, W), U32), pltpu.SemaphoreType.DMA(())],
        compiler_params=_cparams(("arbitrary",)),
        name="moe_combine_ln2",
    )(dest_flat, wt_tok, x1, ys, wg_s, wu_s, wd_s, g, b)


def _rope_tables(positions):
    half = ROT_DIM // 2
    inv_freq = ROPE_THETA ** (-jnp.arange(0, ROT_DIM, 2, dtype=F32) / ROT_DIM)
    ang = positions.reshape(-1).astype(F32)[:, None] * inv_freq
    cos = jnp.cos(ang)
    sin = jnp.sin(ang)
    T = ang.shape[0]
    pad = HEAD_DIM - ROT_DIM
    c64 = jnp.concatenate([cos, cos, jnp.ones((T, pad), F32)], axis=1)
    sa64 = jnp.concatenate([jnp.zeros((T, half), F32), sin, jnp.zeros((T, pad), F32)], axis=1)
    sb64 = jnp.concatenate([-sin, jnp.zeros((T, half + pad), F32)], axis=1)
    rep = LANES // HEAD_DIM
    return jnp.tile(c64, (1, rep)), jnp.tile(sa64, (1, rep)), jnp.tile(sb64, (1, rep))


def kernel(x, positions, w_in, lam_q1, lam_k1, lam_q2, lam_k2, subln_g, sgu_ln_g, sgu_ln_b,
           w_spatial, b_spatial, w_out, ln1_g, ln1_b, w_router, router_bias, w_gate_exp,
           w_up_exp, w_down_exp, w_gate_sh, w_up_sh, w_down_sh, ln2_g, ln2_b):
    B, S, D = x.shape
    T = B * S
    l = 0
    x2 = x.reshape(T, D)
    n_exp = w_router.shape[-1]
    blk = 256

    w_in_b = w_in[l].astype(BF16)
    w_out_b = w_out[l].astype(BF16)
    rc, rsa, rsb = _rope_tables(positions)
    lam = (jnp.exp(jnp.sum(lam_q1[l].astype(F32) * lam_k1[l].astype(F32)))
           - jnp.exp(jnp.sum(lam_q2[l].astype(F32) * lam_k2[l].astype(F32)))
           + LAMBDA_INIT).reshape(1, 1).astype(F32)
    bs_full = jnp.broadcast_to(b_spatial[l][:, :, None], (N_SG_GROUPS, SG_CHUNK, SG_DIM)).astype(F32)
    w_rt = w_router[l].astype(F32).T
    w_rt_hi = w_rt.astype(BF16)
    w_rt_lo = (w_rt - w_rt_hi.astype(F32)).astype(BF16)
    bias_col = router_bias[l].astype(F32).reshape(n_exp, 1)

    proj = _inproj(x2, w_in_b, rc, rsa, rsb)
    attn = _attention(proj, lam, subln_g[l].reshape(1, V_DIM).astype(F32), B, S)
    sg = _sgu(proj, sgu_ln_g[l].astype(F32), sgu_ln_b[l].astype(F32), w_spatial[l], bs_full)
    x1, x1p = _outproj(attn, sg, x2, w_out_b, ln1_g[l].reshape(1, D), ln1_b[l].reshape(1, D))

    eidx, wt, rank, cnt = _router(x1, w_rt_hi, w_rt_lo, bias_col)

    counts = cnt.reshape(n_exp).astype(I32)
    padded = ((counts + blk - 1) // blk) * blk
    pad_end = jnp.cumsum(padded)
    pad_off = pad_end - padded
    dest = jnp.take(pad_off, eidx, axis=0) + rank
    dest_flat = dest.T.reshape(T * TOP_K)
    wt_tok = wt.T
    P = T * TOP_K + n_exp * blk
    nb = P // blk
    block_expert = jnp.clip(
        jnp.searchsorted(pad_end, jnp.arange(nb, dtype=I32) * blk, side="right"),
        0, n_exp - 1).astype(I32)
    nb_used = (pad_end[-1] // blk).reshape(1).astype(I32)

    xs = _dispatch(counts, pad_off.astype(I32), dest_flat, x1p, P, blk)
    ys = _gmm(block_expert, nb_used, xs, w_gate_exp[l], w_up_exp[l], w_down_exp[l], blk)
    out = _combine(dest_flat, wt_tok, x1, ys,
                   w_gate_sh[l].astype(BF16), w_up_sh[l].astype(BF16), w_down_sh[l].astype(BF16),
                   ln2_g[l].reshape(1, D), ln2_b[l].reshape(1, D))
    return out.reshape(B, S, D)
```

```python
import functools
import math

import jax
import jax.numpy as jnp
from jax import lax
from jax.experimental import pallas as pl
from jax.experimental.pallas import tpu as pltpu

F32 = jnp.float32
BF16 = jnp.bfloat16
U32 = jnp.uint32
I32 = jnp.int32

N_HEADS = 8
HEAD_DIM = 64
V_DIM = 2 * HEAD_DIM
N_SG_GROUPS = 8
SG_DIM = 128
SG_CHUNK = 128
ROPE_THETA = 500000.0
ROT_DIM = HEAD_DIM // 4
TOP_K = 8
N_EXPERT_GROUPS = 8
TOPK_GROUPS = 4
ROUTED_SCALE = 2.5
LN_EPS = 1e-5
DEPTH = 1
ALPHA = (2 * DEPTH) ** 0.25
LAMBDA_INIT = 0.8 - 0.6 * math.exp(-0.3 * 0)
QK_SCALE = HEAD_DIM ** -0.5 * math.log2(math.e)

LANES = 128
SEG = 1024
ATTN_TQ = (512, 256, 128)
ATTN_TK = (1024, 512, 256, 128)
VMEM_LIMIT = 56 * 1024 * 1024


def _cparams(sem, vmem=VMEM_LIMIT):
    return pltpu.CompilerParams(dimension_semantics=sem, vmem_limit_bytes=vmem)


def _pick(n, prefs):
    for p in prefs:
        if n % p == 0:
            return p
    return n


def _inproj_kernel(x_ref, w_ref, c_ref, sa_ref, sb_ref, o_ref, xb_ref):
    j = pl.program_id(1)

    @pl.when(j == 0)
    def _():
        xb_ref[...] = x_ref[...].astype(BF16)

    acc = jnp.dot(xb_ref[...], w_ref[...], preferred_element_type=F32)

    @pl.when(j < 2)
    def _():
        scale = jnp.where(j == 0, QK_SCALE, 1.0).astype(F32)
        c = c_ref[...] * scale
        sa = sa_ref[...] * scale
        sb = sb_ref[...] * scale
        for blk in range(SEG // LANES):
            a = acc[:, blk * LANES:(blk + 1) * LANES]
            r = a * c + pltpu.roll(a, 8, 1) * sa + pltpu.roll(a, LANES - 8, 1) * sb
            o_ref[:, blk * LANES:(blk + 1) * LANES] = r.astype(BF16)

    @pl.when(j == 2)
    def _():
        o_ref[...] = acc.astype(BF16)

    @pl.when(j > 2)
    def _():
        g = 0.5 * acc * (1.0 + lax.erf(acc * (2.0 ** -0.5)))
        o_ref[...] = g.astype(BF16)


def _inproj(x2, w_b, rc, rsa, rsb):
    T, D = x2.shape
    n_seg = w_b.shape[1] // SEG
    tm = _pick(T, (512, 256, 128))
    return pl.pallas_call(
        _inproj_kernel,
        grid=(T // tm, n_seg),
        in_specs=[
            pl.BlockSpec((tm, D), lambda i, j: (i, 0)),
            pl.BlockSpec((D, SEG), lambda i, j: (0, j)),
            pl.BlockSpec((tm, LANES), lambda i, j: (i, 0)),
            pl.BlockSpec((tm, LANES), lambda i, j: (i, 0)),
            pl.BlockSpec((tm, LANES), lambda i, j: (i, 0)),
        ],
        out_specs=pl.BlockSpec((tm, SEG), lambda i, j: (i, j)),
        out_shape=jax.ShapeDtypeStruct((T, n_seg * SEG), BF16),
        scratch_shapes=[pltpu.VMEM((tm, D), BF16)],
        compiler_params=_cparams(("arbitrary", "arbitrary")),
        name="inproj",
    )(x2, w_b, rc, rsa, rsb)


def _attn_kernel(lam_ref, q_ref, k_ref, v_ref, g_ref, o_ref,
                 acc_a, acc_b, m_a, m_b, s_a0, s_a1, s_b0, s_b1, *, tq, tk):
    qi = pl.program_id(2)
    s_a = (s_a0, s_a1)
    s_b = (s_b0, s_b1)
    q = q_ref[...]
    lane = lax.broadcasted_iota(I32, q.shape, 1)
    zero = jnp.zeros_like(q)
    q_a = jnp.where(lane < HEAD_DIM, q, zero)
    q_b = jnp.where(lane >= HEAD_DIM, q, zero)

    for r in (acc_a, acc_b):
        r[...] = jnp.zeros(r.shape, F32)
    for r in (m_a, m_b):
        r[...] = jnp.full(r.shape, -jnp.inf, F32)

    def scores(ki, slot):
        start = pl.multiple_of(ki * tk, tk)
        k = k_ref[pl.ds(start, tk), :]
        dn = (((1,), (1,)), ((), ()))
        s_a[slot][...] = lax.dot_general(q_a, k, dn, preferred_element_type=F32)
        s_b[slot][...] = lax.dot_general(q_b, k, dn, preferred_element_type=F32)

    def consume(ki, slot, masked):
        start = pl.multiple_of(ki * tk, tk)
        v = v_ref[pl.ds(start, tk), :]
        v_ext = jnp.concatenate([v, jnp.ones_like(v)], axis=1)
        if masked:
            row = lax.broadcasted_iota(I32, (tq, tk), 0) + (qi * tq - start)
            col = lax.broadcasted_iota(I32, (tq, tk), 1)
            keep = col <= row
        for s_ref, acc, m in ((s_a[slot], acc_a, m_a), (s_b[slot], acc_b, m_b)):
            s = s_ref[...]
            if masked:
                s = jnp.where(keep, s, -jnp.inf)
            m_prev = m[...]
            m_new = jnp.maximum(m_prev, jnp.max(s, axis=1, keepdims=True))
            alpha = jnp.exp2(m_prev - m_new)
            p = jnp.exp2(s - m_new)
            acc[...] = alpha * acc[...] + jnp.dot(p.astype(BF16), v_ext, preferred_element_type=F32)
            m[...] = m_new

    n_full = (qi * tq) // tk
    n_pairs = n_full // 2
    scores(0, 0)

    def body(t, carry):
        ki = 2 * t
        scores(ki + 1, 1)
        consume(ki, 0, False)
        scores(ki + 2, 0)
        consume(ki + 1, 1, False)
        return carry

    lax.fori_loop(0, n_pairs, body, 0)

    @pl.when(n_full % 2 == 1)
    def _():
        scores(n_full, 1)
        consume(n_full - 1, 0, False)
        consume(n_full, 1, True)

    @pl.when(n_full % 2 == 0)
    def _():
        consume(n_full, 0, True)

    lam = lam_ref[0, 0]
    o = (acc_a[:, :V_DIM] / acc_a[:, V_DIM:]) - lam * (acc_b[:, :V_DIM] / acc_b[:, V_DIM:])
    ms = jnp.mean(o * o, axis=1, keepdims=True)
    o = o * lax.rsqrt(ms + LN_EPS) * g_ref[...] * (1.0 - LAMBDA_INIT)
    o_ref[...] = o.astype(BF16)


def _attention(proj, lam, subln_g, B, S):
    T = B * S
    tq = _pick(S, ATTN_TQ)
    nq = S // tq
    tk = _pick(S, ATTN_TK)
    assert tk % tq == 0
    kern = functools.partial(_attn_kernel, tq=tq, tk=tk)
    return pl.pallas_call(
        kern,
        grid=(B, N_HEADS, nq),
        in_specs=[
            pl.BlockSpec(memory_space=pltpu.SMEM),
            pl.BlockSpec((tq, V_DIM), lambda b, h, i: (b * nq + i, h)),
            pl.BlockSpec((S, V_DIM), lambda b, h, i: (b, N_HEADS + h)),
            pl.BlockSpec((S, V_DIM), lambda b, h, i: (b, 2 * N_HEADS + h)),
            pl.BlockSpec((1, V_DIM), lambda b, h, i: (0, 0)),
        ],
        out_specs=pl.BlockSpec((tq, V_DIM), lambda b, h, i: (b * nq + i, h)),
        out_shape=jax.ShapeDtypeStruct((T, N_HEADS * V_DIM), BF16),
        scratch_shapes=[
            pltpu.VMEM((tq, 2 * V_DIM), F32), pltpu.VMEM((tq, 2 * V_DIM), F32),
            pltpu.VMEM((tq, 1), F32), pltpu.VMEM((tq, 1), F32),
            pltpu.VMEM((tq, tk), F32), pltpu.VMEM((tq, tk), F32),
            pltpu.VMEM((tq, tk), F32), pltpu.VMEM((tq, tk), F32),
        ],
        compiler_params=_cparams(("arbitrary", "arbitrary", "arbitrary")),
        name="diff_attention",
    )(lam, proj, proj, proj, subln_g)


def _sgu_kernel(u_ref, v_ref, g_ref, b_ref, w_ref, bs_ref, o_ref, *, tm):
    g = pl.program_id(1)
    ln_g = g_ref[pl.ds(g, 1), :]
    ln_b = b_ref[pl.ds(g, 1), :]
    row = lax.broadcasted_iota(I32, (SG_CHUNK, SG_CHUNK), 0)
    col = lax.broadcasted_iota(I32, (SG_CHUNK, SG_CHUNK), 1)
    w = jnp.where(col <= row, w_ref[0], 0.0).astype(BF16)
    bs = bs_ref[0]
    for c in range(tm // SG_CHUNK):
        sl = slice(c * SG_CHUNK, (c + 1) * SG_CHUNK)
        v = v_ref[sl, :].astype(F32)
        mu = jnp.mean(v, axis=1, keepdims=True)
        vc = v - mu
        var = jnp.mean(vc * vc, axis=1, keepdims=True)
        vn = vc * lax.rsqrt(var + LN_EPS) * ln_g + ln_b
        mixed = jnp.dot(w, vn.astype(BF16), preferred_element_type=F32) + bs
        o_ref[sl, :] = (u_ref[sl, :].astype(F32) * mixed).astype(BF16)


def _sgu(proj, ln_g, ln_b, w_s, bs_full):
    T = proj.shape[0]
    tm = _pick(T, (1024, 512, 256, 128))
    u_blk = 3 * SEG // SG_DIM
    v_blk = 4 * SEG // SG_DIM
    kern = functools.partial(_sgu_kernel, tm=tm)
    return pl.pallas_call(
        kern,
        grid=(T // tm, N_SG_GROUPS),
        in_specs=[
            pl.BlockSpec((tm, SG_DIM), lambda i, g: (i, u_blk + g)),
            pl.BlockSpec((tm, SG_DIM), lambda i, g: (i, v_blk + g)),
            pl.BlockSpec((N_SG_GROUPS, SG_DIM), lambda i, g: (0, 0)),
            pl.BlockSpec((N_SG_GROUPS, SG_DIM), lambda i, g: (0, 0)),
            pl.BlockSpec((1, SG_CHUNK, SG_CHUNK), lambda i, g: (g, 0, 0)),
            pl.BlockSpec((1, SG_CHUNK, SG_DIM), lambda i, g: (g, 0, 0)),
        ],
        out_specs=pl.BlockSpec((tm, SG_DIM), lambda i, g: (i, g)),
        out_shape=jax.ShapeDtypeStruct((T, N_SG_GROUPS * SG_DIM), BF16),
        compiler_params=_cparams(("arbitrary", "arbitrary")),
        name="spatial_gating",
    )(proj, proj, ln_g, ln_b, w_s, bs_full)


def _pack_pairs(y):
    half = y.shape[1] // 2
    lo = lax.bitcast_convert_type(y[:, :half].astype(BF16).astype(F32), U32)
    hi = lax.bitcast_convert_type(y[:, half:].astype(BF16).astype(F32), U32)
    return (lo >> 16) | (hi & jnp.uint32(0xFFFF0000))


def _unpack_pairs(p):
    lo = lax.bitcast_convert_type(p << 16, F32)
    hi = lax.bitcast_convert_type(p & jnp.uint32(0xFFFF0000), F32)
    return lo, hi


ROW_SUB = 8


def _store_rows_as_tiles(ref, packed):
    n = packed.shape[0]
    for s in range(ROW_SUB):
        ref[pl.ds(s, n, stride=ROW_SUB), :] = packed[:, s * LANES:(s + 1) * LANES]


def _load_tiles_as_rows(ref, n):
    return jnp.concatenate(
        [ref[pl.ds(s, n, stride=ROW_SUB), :] for s in range(ROW_SUB)], axis=1)


def _layer_norm(y, g, b):
    mu = jnp.mean(y, axis=1, keepdims=True)
    yc = y - mu
    var = jnp.mean(yc * yc, axis=1, keepdims=True)
    return yc * lax.rsqrt(var + LN_EPS) * g + b


def _outproj_kernel(a_ref, s_ref, x_ref, w_ref, g_ref, b_ref, x1_ref, xp_ref):
    half = w_ref.shape[0] // 2
    mix = jnp.dot(a_ref[...], w_ref[:half, :], preferred_element_type=F32)
    mix = mix + jnp.dot(s_ref[...], w_ref[half:, :], preferred_element_type=F32)
    y = ALPHA * x_ref[...] + mix
    x1 = _layer_norm(y, g_ref[...], b_ref[...])
    x1_ref[...] = x1
    _store_rows_as_tiles(xp_ref, _pack_pairs(x1))


def _outproj(attn, sg, x2, w_b, g, b):
    T, D = x2.shape
    assert D // 2 == ROW_SUB * LANES
    tm = _pick(T, (256, 128))
    wa = attn.shape[1]
    ws = sg.shape[1]
    return pl.pallas_call(
        _outproj_kernel,
        grid=(T // tm,),
        in_specs=[
            pl.BlockSpec((tm, wa), lambda i: (i, 0)),
            pl.BlockSpec((tm, ws), lambda i: (i, 0)),
            pl.BlockSpec((tm, D), lambda i: (i, 0)),
            pl.BlockSpec((wa + ws, D), lambda i: (0, 0)),
            pl.BlockSpec((1, D), lambda i: (0, 0)),
            pl.BlockSpec((1, D), lambda i: (0, 0)),
        ],
        out_specs=[
            pl.BlockSpec((tm, D), lambda i: (i, 0)),
            pl.BlockSpec((tm * ROW_SUB, LANES), lambda i: (i, 0)),
        ],
        out_shape=[
            jax.ShapeDtypeStruct((T, D), F32),
            jax.ShapeDtypeStruct((T * ROW_SUB, LANES), U32),
        ],
        compiler_params=_cparams(("arbitrary",)),
        name="outproj_ln1",
    )(attn, sg, x2, w_b, g, b)


def _router_kernel(x_ref, wh_ref, wl_ref, bias_ref, idx_ref, wt_ref, rank_ref, cnt_ref,
                   carry_ref, *, tm, n_exp):
    step = pl.program_id(0)

    @pl.when(step == 0)
    def _():
        carry_ref[...] = jnp.zeros(carry_ref.shape, F32)

    x = x_ref[...]
    xh = x.astype(BF16)
    xl = (x - xh.astype(F32)).astype(BF16)
    dn = (((1,), (1,)), ((), ()))
    wh = wh_ref[...]
    logits = (lax.dot_general(wh, xh, dn, preferred_element_type=F32)
              + lax.dot_general(wh, xl, dn, preferred_element_type=F32)
              + lax.dot_general(wl_ref[...], xh, dn, preferred_element_type=F32))
    scores = jax.nn.sigmoid(logits)
    choice = scores + bias_ref[...]

    per = n_exp // N_EXPERT_GROUPS
    neg = jnp.float32(-jnp.inf)
    sub = lax.broadcasted_iota(I32, (per, tm), 0)
    gscore = []
    for g in range(N_EXPERT_GROUPS):
        cg = choice[g * per:(g + 1) * per, :]
        m1 = jnp.max(cg, axis=0, keepdims=True)
        i1 = jnp.min(jnp.where(cg == m1, sub, per), axis=0, keepdims=True)
        m2 = jnp.max(jnp.where(sub == i1, neg, cg), axis=0, keepdims=True)
        gscore.append(m1 + m2)
    masked_parts = []
    for g in range(N_EXPERT_GROUPS):
        beat = jnp.zeros((1, tm), I32)
        for h in range(N_EXPERT_GROUPS):
            if h == g:
                continue
            wins = (gscore[h] > gscore[g]) | ((gscore[h] == gscore[g]) & (h < g))
            beat = beat + wins.astype(I32)
        keep = beat < TOPK_GROUPS
        cg = choice[g * per:(g + 1) * per, :]
        masked_parts.append(jnp.where(keep, cg, neg))
    masked = jnp.concatenate(masked_parts, axis=0)

    eidx = lax.broadcasted_iota(I32, (n_exp, tm), 0)
    hot = jnp.zeros((n_exp, tm), F32)
    sel_idx, sel_w, sel_hot = [], [], []
    for _ in range(TOP_K):
        mk = jnp.max(masked, axis=0, keepdims=True)
        ik = jnp.min(jnp.where(masked == mk, eidx, n_exp), axis=0, keepdims=True)
        one = eidx == ik
        sel_idx.append(ik)
        sel_w.append(jnp.sum(jnp.where(one, scores, 0.0), axis=0, keepdims=True))
        sel_hot.append(one)
        hot = hot + one.astype(F32)
        masked = jnp.where(one, neg, masked)

    wsum = sel_w[0]
    for k in range(1, TOP_K):
        wsum = wsum + sel_w[k]

    r = lax.broadcasted_iota(I32, (tm, tm), 0)
    c = lax.broadcasted_iota(I32, (tm, tm), 1)
    upper = (r < c).astype(BF16)
    pref = jnp.dot(hot.astype(BF16), upper, preferred_element_type=F32) + carry_ref[...]
    for k in range(TOP_K):
        idx_ref[k:k + 1, :] = sel_idx[k]
        wt_ref[k:k + 1, :] = sel_w[k] / wsum * ROUTED_SCALE
        rk = jnp.sum(jnp.where(sel_hot[k], pref, 0.0), axis=0, keepdims=True)
        rank_ref[k:k + 1, :] = rk.astype(I32)
    carry_ref[...] = carry_ref[...] + jnp.sum(hot, axis=1, keepdims=True)
    cnt_ref[...] = carry_ref[...]


def _router(x1, wh, wl, bias_col):
    T, D = x1.shape
    n_exp = wh.shape[0]
    tm = _pick(T, (512, 256, 128))
    kern = functools.partial(_router_kernel, tm=tm, n_exp=n_exp)
    return pl.pallas_call(
        kern,
        grid=(T // tm,),
        in_specs=[
            pl.BlockSpec((tm, D), lambda i: (i, 0)),
            pl.BlockSpec((n_exp, D), lambda i: (0, 0)),
            pl.BlockSpec((n_exp, D), lambda i: (0, 0)),
            pl.BlockSpec((n_exp, 1), lambda i: (0, 0)),
        ],
        out_specs=[
            pl.BlockSpec((TOP_K, tm), lambda i: (0, i)),
            pl.BlockSpec((TOP_K, tm), lambda i: (0, i)),
            pl.BlockSpec((TOP_K, tm), lambda i: (0, i)),
            pl.BlockSpec((n_exp, 1), lambda i: (0, 0)),
        ],
        out_shape=[
            jax.ShapeDtypeStruct((TOP_K, T), I32),
            jax.ShapeDtypeStruct((TOP_K, T), F32),
            jax.ShapeDtypeStruct((TOP_K, T), I32),
            jax.ShapeDtypeStruct((n_exp, 1), F32),
        ],
        scratch_shapes=[pltpu.VMEM((n_exp, 1), F32)],
        compiler_params=_cparams(("arbitrary",)),
        name="router_topk",
    )(x1, wh, wl, bias_col)


def _dest_kernel(idx_ref, rank_ref, off_ref, o_ref, *, tm, n_exp):
    eidx = lax.broadcasted_iota(I32, (n_exp, tm), 0)
    off = off_ref[...]
    for k in range(TOP_K):
        hit = eidx == idx_ref[k:k + 1, :]
        base = jnp.sum(jnp.where(hit, off, 0.0), axis=0, keepdims=True)
        o_ref[k:k + 1, :] = base.astype(I32) + rank_ref[k:k + 1, :]


def _dest_rows(eidx, rank, pad_off_col):
    K, T = eidx.shape
    n_exp = pad_off_col.shape[0]
    tm = _pick(T, (1024, 512, 256, 128))
    kern = functools.partial(_dest_kernel, tm=tm, n_exp=n_exp)
    return pl.pallas_call(
        kern,
        grid=(T // tm,),
        in_specs=[
            pl.BlockSpec((K, tm), lambda i: (0, i)),
            pl.BlockSpec((K, tm), lambda i: (0, i)),
            pl.BlockSpec((n_exp, 1), lambda i: (0, 0)),
        ],
        out_specs=pl.BlockSpec((K, tm), lambda i: (0, i)),
        out_shape=jax.ShapeDtypeStruct((K, T), I32),
        compiler_params=_cparams(("arbitrary",)),
        name="moe_dest_rows",
    )(eidx, rank, pad_off_col)


def _dispatch_kernel(cnt_ref, off_ref, dest_ref, x_ref, xs_ref, zero_ref, sem, *,
                     tm, blk, n_exp, exp_per_step):
    step = pl.program_id(0)

    def tile(ref, r):
        return ref.at[pl.ds(pl.multiple_of(r * ROW_SUB, ROW_SUB), ROW_SUB), :]

    def issue(t, carry):
        src = tile(x_ref, t)
        for k in range(TOP_K):
            pltpu.make_async_copy(src, tile(xs_ref, dest_ref[t * TOP_K + k]), sem).start()
        return carry

    lax.fori_loop(0, tm, issue, 0)

    zero_ref[...] = jnp.zeros(zero_ref.shape, U32)

    def zero_copy(d):
        return pltpu.make_async_copy(zero_ref, tile(xs_ref, d), sem)

    n_pad_total = jnp.int32(0)
    for j in range(exp_per_step):
        e = jnp.minimum(step * exp_per_step + j, n_exp - 1)
        valid = (step * exp_per_step + j) < n_exp
        cnt = cnt_ref[e]
        n_pad = jnp.where(valid, (blk - cnt % blk) % blk, 0)
        base = off_ref[e] + cnt

        def zissue(i, carry, base=base):
            zero_copy(base + i).start()
            return carry

        lax.fori_loop(0, n_pad, zissue, 0)
        n_pad_total = n_pad_total + n_pad

    for k in range(TOP_K):
        pltpu.make_async_copy(x_ref, xs_ref.at[pl.ds(0, tm * ROW_SUB), :], sem).wait()

    def wait(i, carry):
        zero_copy(0).wait()
        return carry

    lax.fori_loop(0, n_pad_total, wait, 0)


def _dispatch(counts, pad_off, dest_flat, x1p, n_rows_out, blk):
    T = x1p.shape[0] // ROW_SUB
    n_exp = counts.shape[0]
    tm = _pick(T, (256, 128))
    n_steps = T // tm
    exp_per_step = -(-n_exp // n_steps)
    kern = functools.partial(_dispatch_kernel, tm=tm, blk=blk, n_exp=n_exp,
                             exp_per_step=exp_per_step)
    grid_spec = pltpu.PrefetchScalarGridSpec(
        num_scalar_prefetch=2,
        grid=(n_steps,),
        in_specs=[
            pl.BlockSpec((tm * TOP_K,), lambda i, c, o: (i,), memory_space=pltpu.SMEM),
            pl.BlockSpec((tm * ROW_SUB, LANES), lambda i, c, o: (i, 0)),
        ],
        out_specs=pl.BlockSpec(memory_space=pl.ANY),
        scratch_shapes=[pltpu.VMEM((ROW_SUB, LANES), U32), pltpu.SemaphoreType.DMA(())],
    )
    return pl.pallas_call(
        kern,
        grid_spec=grid_spec,
        out_shape=jax.ShapeDtypeStruct((n_rows_out * ROW_SUB, LANES), U32),
        compiler_params=_cparams(("arbitrary",)),
        name="moe_dispatch",
    )(counts, pad_off, dest_flat, x1p)


def _gmm_kernel(be_ref, nb_ref, xs_ref, wg_ref, wu_ref, wd_ref, ys_ref, wg_b, wu_b, wd_b):
    b = pl.program_id(0)
    prev = be_ref[jnp.maximum(b - 1, 0)]
    fresh = (b == 0) | (be_ref[b] != prev)
    live = b < nb_ref[0]

    @pl.when(live & fresh)
    def _():
        wg_b[...] = wg_ref[0].astype(BF16)
        wu_b[...] = wu_ref[0].astype(BF16)
        wd_b[...] = wd_ref[0].astype(BF16)

    @pl.when(live)
    def _():
        blk = xs_ref.shape[0] // ROW_SUB
        lo, hi = _unpack_pairs(_load_tiles_as_rows(xs_ref, blk))
        lo = lo.astype(BF16)
        hi = hi.astype(BF16)
        half = lo.shape[1]
        g = (jnp.dot(lo, wg_b[:half, :], preferred_element_type=F32)
             + jnp.dot(hi, wg_b[half:, :], preferred_element_type=F32))
        u = (jnp.dot(lo, wu_b[:half, :], preferred_element_type=F32)
             + jnp.dot(hi, wu_b[half:, :], preferred_element_type=F32))
        h = (g * jax.nn.sigmoid(g) * u).astype(BF16)
        y = jnp.dot(h, wd_b[...], preferred_element_type=F32)
        _store_rows_as_tiles(ys_ref, _pack_pairs(y))


def _gmm(block_expert, nb_used, xs, wg, wu, wd, blk):
    P = xs.shape[0] // ROW_SUB
    n_exp, D, FF = wg.shape
    nb = P // blk

    def row_map(b, be, nbu):
        return (jnp.minimum(b, nbu[0] - 1), 0)

    def w_map(b, be, nbu):
        return (be[b], 0, 0)

    grid_spec = pltpu.PrefetchScalarGridSpec(
        num_scalar_prefetch=2,
        grid=(nb,),
        in_specs=[
            pl.BlockSpec((blk * ROW_SUB, LANES), row_map),
            pl.BlockSpec((1, D, FF), w_map),
            pl.BlockSpec((1, D, FF), w_map),
            pl.BlockSpec((1, FF, D), w_map),
        ],
        out_specs=pl.BlockSpec((blk * ROW_SUB, LANES), row_map),
        scratch_shapes=[
            pltpu.VMEM((D, FF), BF16), pltpu.VMEM((D, FF), BF16), pltpu.VMEM((FF, D), BF16),
        ],
    )
    return pl.pallas_call(
        _gmm_kernel,
        grid_spec=grid_spec,
        out_shape=jax.ShapeDtypeStruct((P * ROW_SUB, LANES), U32),
        compiler_params=_cparams(("arbitrary",)),
        name="moe_grouped_swiglu",
    )(block_expert, nb_used, xs, wg, wu, wd)


def _combine_kernel(dest_ref, wt_ref, x_ref, ys_ref, wg_ref, wu_ref, wd_ref, g_ref, b_ref,
                    o_ref, buf, sem, *, tm):
    def tile(ref, r):
        return ref.at[pl.ds(pl.multiple_of(r * ROW_SUB, ROW_SUB), ROW_SUB), :]

    def issue(t, carry):
        for k in range(TOP_K):
            pltpu.make_async_copy(tile(ys_ref, dest_ref[t * TOP_K + k]), tile(buf.at[k], t), sem).start()
        return carry

    lax.fori_loop(0, tm, issue, 0)

    x = x_ref[...]
    xb = x.astype(BF16)
    g = jnp.dot(xb, wg_ref[...], preferred_element_type=F32)
    u = jnp.dot(xb, wu_ref[...], preferred_element_type=F32)
    h = (g * jax.nn.sigmoid(g) * u).astype(BF16)
    shared = jnp.dot(h, wd_ref[...], preferred_element_type=F32)

    for k in range(TOP_K):
        pltpu.make_async_copy(ys_ref.at[pl.ds(0, tm * ROW_SUB), :], buf.at[k], sem).wait()

    half = x.shape[1] // 2
    wt = wt_ref[...]
    r_lo = jnp.zeros((tm, half), F32)
    r_hi = jnp.zeros((tm, half), F32)
    for k in range(TOP_K):
        lo, hi = _unpack_pairs(_load_tiles_as_rows(buf.at[k], tm))
        wk = wt[:, k:k + 1]
        r_lo = r_lo + lo * wk
        r_hi = r_hi + hi * wk
    routed = jnp.concatenate([r_lo, r_hi], axis=1)
    y = ALPHA * x + (routed + shared)
    o_ref[...] = _layer_norm(y, g_ref[...], b_ref[...])


def _combine(dest_flat, wt_tok, x1, ys, wg_s, wu_s, wd_s, g, b):
    T, D = x1.shape
    FF = wg_s.shape[1]
    tm = _pick(T, (256, 128))
    kern = functools.partial(_combine_kernel, tm=tm)
    return pl.pallas_call(
        kern,
        grid=(T // tm,),
        in_specs=[
            pl.BlockSpec((tm * TOP_K,), lambda i: (i,), memory_space=pltpu.SMEM),
            pl.BlockSpec((tm, TOP_K), lambda i: (i, 0)),
            pl.BlockSpec((tm, D), lambda i: (i, 0)),
            pl.BlockSpec(memory_space=pl.ANY),
            pl.BlockSpec((D, FF), lambda i: (0, 0)),
            pl.BlockSpec((D, FF), lambda i: (0, 0)),
            pl.BlockSpec((FF, D), lambda i: (0, 0)),
            pl.BlockSpec((1, D), lambda i: (0, 0)),
            pl.BlockSpec((1, D), lambda i: (0, 0)),
        ],
        out_specs=pl.BlockSpec((tm, D), lambda i: (i, 0)),
        out_shape=jax.ShapeDtypeStruct((T, D), F32),
        scratch_shapes=[pltpu.VMEM((TOP_K, tm * ROW_SUB, LANES), U32), pltpu.SemaphoreType.DMA(())],
        compiler_params=_cparams(("arbitrary",)),
        name="moe_combine_ln2",
    )(dest_flat, wt_tok, x1, ys, wg_s, wu_s, wd_s, g, b)


def _rope_tables(positions):
    half = ROT_DIM // 2
    inv_freq = ROPE_THETA ** (-jnp.arange(0, ROT_DIM, 2, dtype=F32) / ROT_DIM)
    ang = positions.reshape(-1).astype(F32)[:, None] * inv_freq
    cos = jnp.cos(ang)
    sin = jnp.sin(ang)
    T = ang.shape[0]
    pad = HEAD_DIM - ROT_DIM
    c64 = jnp.concatenate([cos, cos, jnp.ones((T, pad), F32)], axis=1)
    sa64 = jnp.concatenate([jnp.zeros((T, half), F32), sin, jnp.zeros((T, pad), F32)], axis=1)
    sb64 = jnp.concatenate([-sin, jnp.zeros((T, half + pad), F32)], axis=1)
    rep = LANES // HEAD_DIM
    return jnp.tile(c64, (1, rep)), jnp.tile(sa64, (1, rep)), jnp.tile(sb64, (1, rep))


def kernel(x, positions, w_in, lam_q1, lam_k1, lam_q2, lam_k2, subln_g, sgu_ln_g, sgu_ln_b,
           w_spatial, b_spatial, w_out, ln1_g, ln1_b, w_router, router_bias, w_gate_exp,
           w_up_exp, w_down_exp, w_gate_sh, w_up_sh, w_down_sh, ln2_g, ln2_b):
    B, S, D = x.shape
    T = B * S
    l = 0
    x2 = x.reshape(T, D)
    n_exp = w_router.shape[-1]
    blk = 256

    w_in_b = w_in[l].astype(BF16)
    w_out_b = w_out[l].astype(BF16)
    rc, rsa, rsb = _rope_tables(positions)
    lam = (jnp.exp(jnp.sum(lam_q1[l].astype(F32) * lam_k1[l].astype(F32)))
           - jnp.exp(jnp.sum(lam_q2[l].astype(F32) * lam_k2[l].astype(F32)))
           + LAMBDA_INIT).reshape(1, 1).astype(F32)
    bs_full = jnp.broadcast_to(b_spatial[l][:, :, None], (N_SG_GROUPS, SG_CHUNK, SG_DIM)).astype(F32)
    w_rt = w_router[l].astype(F32).T
    w_rt_hi = w_rt.astype(BF16)
    w_rt_lo = (w_rt - w_rt_hi.astype(F32)).astype(BF16)
    bias_col = router_bias[l].astype(F32).reshape(n_exp, 1)

    proj = _inproj(x2, w_in_b, rc, rsa, rsb)
    attn = _attention(proj, lam, subln_g[l].reshape(1, V_DIM).astype(F32), B, S)
    sg = _sgu(proj, sgu_ln_g[l].astype(F32), sgu_ln_b[l].astype(F32), w_spatial[l], bs_full)
    x1, x1p = _outproj(attn, sg, x2, w_out_b, ln1_g[l].reshape(1, D), ln1_b[l].reshape(1, D))

    eidx, wt, rank, cnt = _router(x1, w_rt_hi, w_rt_lo, bias_col)

    counts = cnt.reshape(n_exp).astype(I32)
    padded = ((counts + blk - 1) // blk) * blk
    pad_end = jnp.cumsum(padded)
    pad_off = pad_end - padded
    P = T * TOP_K + n_exp * blk
    assert P < 2 ** 24
    dest = _dest_rows(eidx, rank, pad_off.astype(F32).reshape(n_exp, 1))
    dest_flat = dest.T.reshape(T * TOP_K)
    wt_tok = wt.T
    nb = P // blk
    block_expert = jnp.clip(
        jnp.searchsorted(pad_end, jnp.arange(nb, dtype=I32) * blk, side="right"),
        0, n_exp - 1).astype(I32)
    nb_used = (pad_end[-1] // blk).reshape(1).astype(I32)

    xs = _dispatch(counts, pad_off.astype(I32), dest_flat, x1p, P, blk)
    ys = _gmm(block_expert, nb_used, xs, w_gate_exp[l], w_up_exp[l], w_down_exp[l], blk)
    out = _combine(dest_flat, wt_tok, x1, ys,
                   w_gate_sh[l].astype(BF16), w_up_sh[l].astype(BF16), w_down_sh[l].astype(BF16),
                   ln2_g[l].reshape(1, D), ln2_b[l].reshape(1, D))
    return out.reshape(B, S, D)
```

```python
import functools
import math

import jax
import jax.numpy as jnp
from jax import lax
from jax.experimental import pallas as pl
from jax.experimental.pallas import tpu as pltpu

F32 = jnp.float32
BF16 = jnp.bfloat16
U32 = jnp.uint32
I32 = jnp.int32

N_HEADS = 8
HEAD_DIM = 64
V_DIM = 2 * HEAD_DIM
N_SG_GROUPS = 8
SG_DIM = 128
SG_CHUNK = 128
ROPE_THETA = 500000.0
ROT_DIM = HEAD_DIM // 4
TOP_K = 8
N_EXPERT_GROUPS = 8
TOPK_GROUPS = 4
ROUTED_SCALE = 2.5
LN_EPS = 1e-5
DEPTH = 1
ALPHA = (2 * DEPTH) ** 0.25
LAMBDA_INIT = 0.8 - 0.6 * math.exp(-0.3 * 0)
QK_SCALE = HEAD_DIM ** -0.5 * math.log2(math.e)

LANES = 128
SEG = 1024
ATTN_TQ = (512, 256, 128)
ATTN_TK = (1024, 512, 256, 128)
VMEM_LIMIT = 56 * 1024 * 1024


def _cparams(sem, vmem=VMEM_LIMIT):
    return pltpu.CompilerParams(dimension_semantics=sem, vmem_limit_bytes=vmem)


def _pick(n, prefs):
    for p in prefs:
        if n % p == 0:
            return p
    return n


def _inproj_kernel(x_ref, w_ref, c_ref, sa_ref, sb_ref, o_ref, xb_ref):
    j = pl.program_id(1)

    @pl.when(j == 0)
    def _():
        xb_ref[...] = x_ref[...].astype(BF16)

    acc = jnp.dot(xb_ref[...], w_ref[...], preferred_element_type=F32)

    @pl.when(j < 2)
    def _():
        scale = jnp.where(j == 0, QK_SCALE, 1.0).astype(F32)
        c = c_ref[...] * scale
        sa = sa_ref[...] * scale
        sb = sb_ref[...] * scale
        for blk in range(SEG // LANES):
            a = acc[:, blk * LANES:(blk + 1) * LANES]
            r = a * c + pltpu.roll(a, 8, 1) * sa + pltpu.roll(a, LANES - 8, 1) * sb
            o_ref[:, blk * LANES:(blk + 1) * LANES] = r.astype(BF16)

    @pl.when(j == 2)
    def _():
        o_ref[...] = acc.astype(BF16)

    @pl.when(j > 2)
    def _():
        g = 0.5 * acc * (1.0 + lax.erf(acc * (2.0 ** -0.5)))
        o_ref[...] = g.astype(BF16)


def _inproj(x2, w_b, rc, rsa, rsb):
    T, D = x2.shape
    n_seg = w_b.shape[1] // SEG
    tm = _pick(T, (512, 256, 128))
    return pl.pallas_call(
        _inproj_kernel,
        grid=(T // tm, n_seg),
        in_specs=[
            pl.BlockSpec((tm, D), lambda i, j: (i, 0)),
            pl.BlockSpec((D, SEG), lambda i, j: (0, j)),
            pl.BlockSpec((tm, LANES), lambda i, j: (i, 0)),
            pl.BlockSpec((tm, LANES), lambda i, j: (i, 0)),
            pl.BlockSpec((tm, LANES), lambda i, j: (i, 0)),
        ],
        out_specs=pl.BlockSpec((tm, SEG), lambda i, j: (i, j)),
        out_shape=jax.ShapeDtypeStruct((T, n_seg * SEG), BF16),
        scratch_shapes=[pltpu.VMEM((tm, D), BF16)],
        compiler_params=_cparams(("arbitrary", "arbitrary")),
        name="inproj",
    )(x2, w_b, rc, rsa, rsb)


def _attn_kernel(lam_ref, q_ref, k_ref, v_ref, g_ref, o_ref,
                 acc_a, acc_b, m_a, m_b, s_a0, s_a1, s_b0, s_b1, *, tq, tk):
    qi = pl.program_id(2)
    s_a = (s_a0, s_a1)
    s_b = (s_b0, s_b1)
    q = q_ref[...]
    lane = lax.broadcasted_iota(I32, q.shape, 1)
    zero = jnp.zeros_like(q)
    q_a = jnp.where(lane < HEAD_DIM, q, zero)
    q_b = jnp.where(lane >= HEAD_DIM, q, zero)

    for r in (acc_a, acc_b):
        r[...] = jnp.zeros(r.shape, F32)
    for r in (m_a, m_b):
        r[...] = jnp.full(r.shape, -jnp.inf, F32)

    def scores(ki, slot):
        start = pl.multiple_of(ki * tk, tk)
        k = k_ref[pl.ds(start, tk), :]
        dn = (((1,), (1,)), ((), ()))
        s_a[slot][...] = lax.dot_general(q_a, k, dn, preferred_element_type=F32)
        s_b[slot][...] = lax.dot_general(q_b, k, dn, preferred_element_type=F32)

    def consume(ki, slot, masked):
        start = pl.multiple_of(ki * tk, tk)
        v = v_ref[pl.ds(start, tk), :]
        v_ext = jnp.concatenate([v, jnp.ones_like(v)], axis=1)
        if masked:
            row = lax.broadcasted_iota(I32, (tq, tk), 0) + (qi * tq - start)
            col = lax.broadcasted_iota(I32, (tq, tk), 1)
            keep = col <= row
        for s_ref, acc, m in ((s_a[slot], acc_a, m_a), (s_b[slot], acc_b, m_b)):
            s = s_ref[...]
            if masked:
                s = jnp.where(keep, s, -jnp.inf)
            m_prev = m[...]
            m_new = jnp.maximum(m_prev, jnp.max(s, axis=1, keepdims=True))
            alpha = jnp.exp2(m_prev - m_new)
            p = jnp.exp2(s - m_new)
            acc[...] = alpha * acc[...] + jnp.dot(p.astype(BF16), v_ext, preferred_element_type=F32)
            m[...] = m_new

    n_full = (qi * tq) // tk
    n_pairs = n_full // 2
    scores(0, 0)

    def body(t, carry):
        ki = 2 * t
        scores(ki + 1, 1)
        consume(ki, 0, False)
        scores(ki + 2, 0)
        consume(ki + 1, 1, False)
        return carry

    lax.fori_loop(0, n_pairs, body, 0)

    @pl.when(n_full % 2 == 1)
    def _():
        scores(n_full, 1)
        consume(n_full - 1, 0, False)
        consume(n_full, 1, True)

    @pl.when(n_full % 2 == 0)
    def _():
        consume(n_full, 0, True)

    lam = lam_ref[0, 0]
    o = (acc_a[:, :V_DIM] / acc_a[:, V_DIM:]) - lam * (acc_b[:, :V_DIM] / acc_b[:, V_DIM:])
    ms = jnp.mean(o * o, axis=1, keepdims=True)
    o = o * lax.rsqrt(ms + LN_EPS) * g_ref[...] * (1.0 - LAMBDA_INIT)
    o_ref[...] = o.astype(BF16)


def _attention(proj, lam, subln_g, B, S):
    T = B * S
    tq = _pick(S, ATTN_TQ)
    nq = S // tq
    tk = _pick(S, ATTN_TK)
    assert tk % tq == 0
    kern = functools.partial(_attn_kernel, tq=tq, tk=tk)
    return pl.pallas_call(
        kern,
        grid=(B, N_HEADS, nq),
        in_specs=[
            pl.BlockSpec(memory_space=pltpu.SMEM),
            pl.BlockSpec((tq, V_DIM), lambda b, h, i: (b * nq + i, h)),
            pl.BlockSpec((S, V_DIM), lambda b, h, i: (b, N_HEADS + h)),
            pl.BlockSpec((S, V_DIM), lambda b, h, i: (b, 2 * N_HEADS + h)),
            pl.BlockSpec((1, V_DIM), lambda b, h, i: (0, 0)),
        ],
        out_specs=pl.BlockSpec((tq, V_DIM), lambda b, h, i: (b * nq + i, h)),
        out_shape=jax.ShapeDtypeStruct((T, N_HEADS * V_DIM), BF16),
        scratch_shapes=[
            pltpu.VMEM((tq, 2 * V_DIM), F32), pltpu.VMEM((tq, 2 * V_DIM), F32),
            pltpu.VMEM((tq, 1), F32), pltpu.VMEM((tq, 1), F32),
            pltpu.VMEM((tq, tk), F32), pltpu.VMEM((tq, tk), F32),
            pltpu.VMEM((tq, tk), F32), pltpu.VMEM((tq, tk), F32),
        ],
        compiler_params=_cparams(("arbitrary", "arbitrary", "arbitrary")),
        name="diff_attention",
    )(lam, proj, proj, proj, subln_g)


def _sgu_kernel(u_ref, v_ref, g_ref, b_ref, w_ref, bs_ref, o_ref, *, tm):
    g = pl.program_id(1)
    ln_g = g_ref[pl.ds(g, 1), :]
    ln_b = b_ref[pl.ds(g, 1), :]
    row = lax.broadcasted_iota(I32, (SG_CHUNK, SG_CHUNK), 0)
    col = lax.broadcasted_iota(I32, (SG_CHUNK, SG_CHUNK), 1)
    w = jnp.where(col <= row, w_ref[0], 0.0).astype(BF16)
    bs = bs_ref[0]
    for c in range(tm // SG_CHUNK):
        sl = slice(c * SG_CHUNK, (c + 1) * SG_CHUNK)
        v = v_ref[sl, :].astype(F32)
        mu = jnp.mean(v, axis=1, keepdims=True)
        vc = v - mu
        var = jnp.mean(vc * vc, axis=1, keepdims=True)
        vn = vc * lax.rsqrt(var + LN_EPS) * ln_g + ln_b
        mixed = jnp.dot(w, vn.astype(BF16), preferred_element_type=F32) + bs
        o_ref[sl, :] = (u_ref[sl, :].astype(F32) * mixed).astype(BF16)


def _sgu(proj, ln_g, ln_b, w_s, bs_full):
    T = proj.shape[0]
    tm = _pick(T, (1024, 512, 256, 128))
    u_blk = 3 * SEG // SG_DIM
    v_blk = 4 * SEG // SG_DIM
    kern = functools.partial(_sgu_kernel, tm=tm)
    return pl.pallas_call(
        kern,
        grid=(T // tm, N_SG_GROUPS),
        in_specs=[
            pl.BlockSpec((tm, SG_DIM), lambda i, g: (i, u_blk + g)),
            pl.BlockSpec((tm, SG_DIM), lambda i, g: (i, v_blk + g)),
            pl.BlockSpec((N_SG_GROUPS, SG_DIM), lambda i, g: (0, 0)),
            pl.BlockSpec((N_SG_GROUPS, SG_DIM), lambda i, g: (0, 0)),
            pl.BlockSpec((1, SG_CHUNK, SG_CHUNK), lambda i, g: (g, 0, 0)),
            pl.BlockSpec((1, SG_CHUNK, SG_DIM), lambda i, g: (g, 0, 0)),
        ],
        out_specs=pl.BlockSpec((tm, SG_DIM), lambda i, g: (i, g)),
        out_shape=jax.ShapeDtypeStruct((T, N_SG_GROUPS * SG_DIM), BF16),
        compiler_params=_cparams(("arbitrary", "arbitrary")),
        name="spatial_gating",
    )(proj, proj, ln_g, ln_b, w_s, bs_full)


def _pack_halves(y_lo, y_hi):
    lo = lax.bitcast_convert_type(y_lo.astype(BF16).astype(F32), U32)
    hi = lax.bitcast_convert_type(y_hi.astype(BF16).astype(F32), U32)
    return (lo >> 16) | (hi & jnp.uint32(0xFFFF0000))


def _pack_pairs(y):
    half = y.shape[1] // 2
    return _pack_halves(y[:, :half], y[:, half:])


def _unpack_pairs(p):
    lo = lax.bitcast_convert_type(p << 16, F32)
    hi = lax.bitcast_convert_type(p & jnp.uint32(0xFFFF0000), F32)
    return lo, hi


ROW_SUB = 8


def _store_rows_as_tiles(ref, packed):
    n = packed.shape[0]
    for s in range(ROW_SUB):
        ref[pl.ds(s, n, stride=ROW_SUB), :] = packed[:, s * LANES:(s + 1) * LANES]


def _load_tiles_as_rows(ref, n):
    return jnp.concatenate(
        [ref[pl.ds(s, n, stride=ROW_SUB), :] for s in range(ROW_SUB)], axis=1)


def _layer_norm(y, g, b):
    mu = jnp.mean(y, axis=1, keepdims=True)
    yc = y - mu
    var = jnp.mean(yc * yc, axis=1, keepdims=True)
    return yc * lax.rsqrt(var + LN_EPS) * g + b


def _outproj_kernel(a_ref, s_ref, x_ref, w_ref, g_ref, b_ref, x1_ref, xp_ref):
    half = w_ref.shape[0] // 2
    mix = jnp.dot(a_ref[...], w_ref[:half, :], preferred_element_type=F32)
    mix = mix + jnp.dot(s_ref[...], w_ref[half:, :], preferred_element_type=F32)
    y = ALPHA * x_ref[...] + mix
    x1 = _layer_norm(y, g_ref[...], b_ref[...])
    x1_ref[...] = x1
    _store_rows_as_tiles(xp_ref, _pack_pairs(x1))


def _outproj(attn, sg, x2, w_b, g, b):
    T, D = x2.shape
    assert D // 2 == ROW_SUB * LANES
    tm = _pick(T, (256, 128))
    wa = attn.shape[1]
    ws = sg.shape[1]
    return pl.pallas_call(
        _outproj_kernel,
        grid=(T // tm,),
        in_specs=[
            pl.BlockSpec((tm, wa), lambda i: (i, 0)),
            pl.BlockSpec((tm, ws), lambda i: (i, 0)),
            pl.BlockSpec((tm, D), lambda i: (i, 0)),
            pl.BlockSpec((wa + ws, D), lambda i: (0, 0)),
            pl.BlockSpec((1, D), lambda i: (0, 0)),
            pl.BlockSpec((1, D), lambda i: (0, 0)),
        ],
        out_specs=[
            pl.BlockSpec((tm, D), lambda i: (i, 0)),
            pl.BlockSpec((tm * ROW_SUB, LANES), lambda i: (i, 0)),
        ],
        out_shape=[
            jax.ShapeDtypeStruct((T, D), F32),
            jax.ShapeDtypeStruct((T * ROW_SUB, LANES), U32),
        ],
        compiler_params=_cparams(("arbitrary",)),
        name="outproj_ln1",
    )(attn, sg, x2, w_b, g, b)


def _router_kernel(x_ref, wh_ref, wl_ref, bias_ref, idx_ref, wt_ref, rank_ref, cnt_ref,
                   carry_ref, *, tm, n_exp):
    step = pl.program_id(0)

    @pl.when(step == 0)
    def _():
        carry_ref[...] = jnp.zeros(carry_ref.shape, F32)

    x = x_ref[...]
    xh = x.astype(BF16)
    xl = (x - xh.astype(F32)).astype(BF16)
    dn = (((1,), (1,)), ((), ()))
    wh = wh_ref[...]
    logits = (lax.dot_general(wh, xh, dn, preferred_element_type=F32)
              + lax.dot_general(wh, xl, dn, preferred_element_type=F32)
              + lax.dot_general(wl_ref[...], xh, dn, preferred_element_type=F32))
    scores = jax.nn.sigmoid(logits)
    choice = scores + bias_ref[...]

    per = n_exp // N_EXPERT_GROUPS
    neg = jnp.float32(-jnp.inf)
    sub = lax.broadcasted_iota(I32, (per, tm), 0)
    gscore = []
    for g in range(N_EXPERT_GROUPS):
        cg = choice[g * per:(g + 1) * per, :]
        m1 = jnp.max(cg, axis=0, keepdims=True)
        i1 = jnp.min(jnp.where(cg == m1, sub, per), axis=0, keepdims=True)
        m2 = jnp.max(jnp.where(sub == i1, neg, cg), axis=0, keepdims=True)
        gscore.append(m1 + m2)
    masked_parts = []
    for g in range(N_EXPERT_GROUPS):
        beat = jnp.zeros((1, tm), I32)
        for h in range(N_EXPERT_GROUPS):
            if h == g:
                continue
            wins = (gscore[h] > gscore[g]) | ((gscore[h] == gscore[g]) & (h < g))
            beat = beat + wins.astype(I32)
        keep = beat < TOPK_GROUPS
        cg = choice[g * per:(g + 1) * per, :]
        masked_parts.append(jnp.where(keep, cg, neg))
    masked = jnp.concatenate(masked_parts, axis=0)

    eidx = lax.broadcasted_iota(I32, (n_exp, tm), 0)
    hot = jnp.zeros((n_exp, tm), F32)
    sel_idx, sel_w, sel_hot = [], [], []
    for _ in range(TOP_K):
        mk = jnp.max(masked, axis=0, keepdims=True)
        ik = jnp.min(jnp.where(masked == mk, eidx, n_exp), axis=0, keepdims=True)
        one = eidx == ik
        sel_idx.append(ik)
        sel_w.append(jnp.sum(jnp.where(one, scores, 0.0), axis=0, keepdims=True))
        sel_hot.append(one)
        hot = hot + one.astype(F32)
        masked = jnp.where(one, neg, masked)

    wsum = sel_w[0]
    for k in range(1, TOP_K):
        wsum = wsum + sel_w[k]

    r = lax.broadcasted_iota(I32, (tm, tm), 0)
    c = lax.broadcasted_iota(I32, (tm, tm), 1)
    upper = (r < c).astype(BF16)
    pref = jnp.dot(hot.astype(BF16), upper, preferred_element_type=F32) + carry_ref[...]
    for k in range(TOP_K):
        idx_ref[k:k + 1, :] = sel_idx[k]
        wt_ref[k:k + 1, :] = sel_w[k] / wsum * ROUTED_SCALE
        rk = jnp.sum(jnp.where(sel_hot[k], pref, 0.0), axis=0, keepdims=True)
        rank_ref[k:k + 1, :] = rk.astype(I32)
    carry_ref[...] = carry_ref[...] + jnp.sum(hot, axis=1, keepdims=True)
    cnt_ref[...] = carry_ref[...]


def _router(x1, wh, wl, bias_col):
    T, D = x1.shape
    n_exp = wh.shape[0]
    tm = _pick(T, (512, 256, 128))
    kern = functools.partial(_router_kernel, tm=tm, n_exp=n_exp)
    return pl.pallas_call(
        kern,
        grid=(T // tm,),
        in_specs=[
            pl.BlockSpec((tm, D), lambda i: (i, 0)),
            pl.BlockSpec((n_exp, D), lambda i: (0, 0)),
            pl.BlockSpec((n_exp, D), lambda i: (0, 0)),
            pl.BlockSpec((n_exp, 1), lambda i: (0, 0)),
        ],
        out_specs=[
            pl.BlockSpec((TOP_K, tm), lambda i: (0, i)),
            pl.BlockSpec((TOP_K, tm), lambda i: (0, i)),
            pl.BlockSpec((TOP_K, tm), lambda i: (0, i)),
            pl.BlockSpec((n_exp, 1), lambda i: (0, 0)),
        ],
        out_shape=[
            jax.ShapeDtypeStruct((TOP_K, T), I32),
            jax.ShapeDtypeStruct((TOP_K, T), F32),
            jax.ShapeDtypeStruct((TOP_K, T), I32),
            jax.ShapeDtypeStruct((n_exp, 1), F32),
        ],
        scratch_shapes=[pltpu.VMEM((n_exp, 1), F32)],
        compiler_params=_cparams(("arbitrary",)),
        name="router_topk",
    )(x1, wh, wl, bias_col)


def _dest_kernel(idx_ref, rank_ref, off_ref, o_ref, *, tm, n_exp):
    eidx = lax.broadcasted_iota(I32, (n_exp, tm), 0)
    off = off_ref[...]
    for k in range(TOP_K):
        hit = eidx == idx_ref[k:k + 1, :]
        base = jnp.sum(jnp.where(hit, off, 0.0), axis=0, keepdims=True)
        o_ref[k:k + 1, :] = base.astype(I32) + rank_ref[k:k + 1, :]


def _dest_rows(eidx, rank, pad_off_col):
    K, T = eidx.shape
    n_exp = pad_off_col.shape[0]
    tm = _pick(T, (1024, 512, 256, 128))
    kern = functools.partial(_dest_kernel, tm=tm, n_exp=n_exp)
    return pl.pallas_call(
        kern,
        grid=(T // tm,),
        in_specs=[
            pl.BlockSpec((K, tm), lambda i: (0, i)),
            pl.BlockSpec((K, tm), lambda i: (0, i)),
            pl.BlockSpec((n_exp, 1), lambda i: (0, 0)),
        ],
        out_specs=pl.BlockSpec((K, tm), lambda i: (0, i)),
        out_shape=jax.ShapeDtypeStruct((K, T), I32),
        compiler_params=_cparams(("arbitrary",)),
        name="moe_dest_rows",
    )(eidx, rank, pad_off_col)


def _dispatch_kernel(cnt_ref, off_ref, dest_ref, x_ref, xs_ref, zero_ref, sem, *,
                     tm, blk, n_exp, exp_per_step):
    step = pl.program_id(0)

    def tile(ref, r):
        return ref.at[pl.ds(pl.multiple_of(r * ROW_SUB, ROW_SUB), ROW_SUB), :]

    def issue(t, carry):
        src = tile(x_ref, t)
        for k in range(TOP_K):
            pltpu.make_async_copy(src, tile(xs_ref, dest_ref[t * TOP_K + k]), sem).start(priority=k % 2)
        return carry

    lax.fori_loop(0, tm, issue, 0)

    zero_ref[...] = jnp.zeros(zero_ref.shape, U32)

    def zero_copy(d):
        return pltpu.make_async_copy(zero_ref, tile(xs_ref, d), sem)

    n_pad_total = jnp.int32(0)
    for j in range(exp_per_step):
        e = jnp.minimum(step * exp_per_step + j, n_exp - 1)
        valid = (step * exp_per_step + j) < n_exp
        cnt = cnt_ref[e]
        n_pad = jnp.where(valid, (blk - cnt % blk) % blk, 0)
        base = off_ref[e] + cnt

        def zissue(i, carry, base=base):
            zero_copy(base + i).start()
            return carry

        lax.fori_loop(0, n_pad, zissue, 0)
        n_pad_total = n_pad_total + n_pad

    for k in range(TOP_K):
        pltpu.make_async_copy(x_ref, xs_ref.at[pl.ds(0, tm * ROW_SUB), :], sem).wait()

    def wait(i, carry):
        zero_copy(0).wait()
        return carry

    lax.fori_loop(0, n_pad_total, wait, 0)


def _dispatch(counts, pad_off, dest_flat, x1p, n_rows_out, blk):
    T = x1p.shape[0] // ROW_SUB
    n_exp = counts.shape[0]
    tm = _pick(T, (256, 128))
    n_steps = T // tm
    exp_per_step = -(-n_exp // n_steps)
    kern = functools.partial(_dispatch_kernel, tm=tm, blk=blk, n_exp=n_exp,
                             exp_per_step=exp_per_step)
    grid_spec = pltpu.PrefetchScalarGridSpec(
        num_scalar_prefetch=2,
        grid=(n_steps,),
        in_specs=[
            pl.BlockSpec((tm * TOP_K,), lambda i, c, o: (i,), memory_space=pltpu.SMEM),
            pl.BlockSpec((tm * ROW_SUB, LANES), lambda i, c, o: (i, 0)),
        ],
        out_specs=pl.BlockSpec(memory_space=pl.ANY),
        scratch_shapes=[pltpu.VMEM((ROW_SUB, LANES), U32), pltpu.SemaphoreType.DMA(())],
    )
    return pl.pallas_call(
        kern,
        grid_spec=grid_spec,
        out_shape=jax.ShapeDtypeStruct((n_rows_out * ROW_SUB, LANES), U32),
        compiler_params=_cparams(("arbitrary",)),
        name="moe_dispatch",
    )(counts, pad_off, dest_flat, x1p)


def _gmm_kernel(be_ref, nb_ref, fresh_ref, slot_ref, next_ref,
                xs_ref, wg_ref, wu_ref, wd_ref, ys_ref,
                wg_f, wu_f, wd_f, wg_b, wu_b, wd_b, sem):
    b = pl.program_id(0)
    live = b < nb_ref[0]

    def fetch(e, s):
        return (pltpu.make_async_copy(wg_ref.at[e], wg_f.at[s], sem.at[s]),
                pltpu.make_async_copy(wu_ref.at[e], wu_f.at[s], sem.at[s]),
                pltpu.make_async_copy(wd_ref.at[e], wd_f.at[s], sem.at[s]))

    @pl.when(b == 0)
    def _():
        for c in fetch(be_ref[0], 0):
            c.start()

    @pl.when(live & (fresh_ref[b] == 1))
    def _():
        s = slot_ref[b]
        for c in fetch(be_ref[b], s):
            c.wait()
        nxt = next_ref[b]

        @pl.when(nxt >= 0)
        def _():
            for c in fetch(nxt, 1 - s):
                c.start()

        wg_b[...] = wg_f[s].astype(BF16)
        wu_b[...] = wu_f[s].astype(BF16)
        wd_b[...] = wd_f[s].astype(BF16)

    @pl.when(live)
    def _():
        blk = xs_ref.shape[0] // ROW_SUB
        lo, hi = _unpack_pairs(_load_tiles_as_rows(xs_ref, blk))
        lo = lo.astype(BF16)
        hi = hi.astype(BF16)
        half = lo.shape[1]
        g = (jnp.dot(lo, wg_b[:half, :], preferred_element_type=F32)
             + jnp.dot(hi, wg_b[half:, :], preferred_element_type=F32))
        u = (jnp.dot(lo, wu_b[:half, :], preferred_element_type=F32)
             + jnp.dot(hi, wu_b[half:, :], preferred_element_type=F32))
        h = (g * jax.nn.sigmoid(g) * u).astype(BF16)
        y_lo = jnp.dot(h, wd_b[:, :half], preferred_element_type=F32)
        y_hi = jnp.dot(h, wd_b[:, half:], preferred_element_type=F32)
        _store_rows_as_tiles(ys_ref, _pack_halves(y_lo, y_hi))


def _gmm(block_expert, nb_used, fresh, slot, next_expert, xs, wg, wu, wd, blk):
    P = xs.shape[0] // ROW_SUB
    n_exp, D, FF = wg.shape
    nb = P // blk

    def row_map(b, be, nbu, fr, sl, nx):
        return (jnp.minimum(b, nbu[0] - 1), 0)

    grid_spec = pltpu.PrefetchScalarGridSpec(
        num_scalar_prefetch=5,
        grid=(nb,),
        in_specs=[
            pl.BlockSpec((blk * ROW_SUB, LANES), row_map),
            pl.BlockSpec(memory_space=pl.ANY),
            pl.BlockSpec(memory_space=pl.ANY),
            pl.BlockSpec(memory_space=pl.ANY),
        ],
        out_specs=pl.BlockSpec((blk * ROW_SUB, LANES), row_map),
        scratch_shapes=[
            pltpu.VMEM((2, D, FF), F32), pltpu.VMEM((2, D, FF), F32), pltpu.VMEM((2, FF, D), F32),
            pltpu.VMEM((D, FF), BF16), pltpu.VMEM((D, FF), BF16), pltpu.VMEM((FF, D), BF16),
            pltpu.SemaphoreType.DMA((2,)),
        ],
    )
    return pl.pallas_call(
        _gmm_kernel,
        grid_spec=grid_spec,
        out_shape=jax.ShapeDtypeStruct((P * ROW_SUB, LANES), U32),
        compiler_params=_cparams(("arbitrary",)),
        name="moe_grouped_swiglu",
    )(block_expert, nb_used, fresh, slot, next_expert, xs, wg, wu, wd)


def _combine_kernel(dest_ref, wt_ref, x_ref, ys_ref, wg_ref, wu_ref, wd_ref, g_ref, b_ref,
                    o_ref, buf, sem, *, tm):
    def tile(ref, r):
        return ref.at[pl.ds(pl.multiple_of(r * ROW_SUB, ROW_SUB), ROW_SUB), :]

    def issue(t, carry):
        for k in range(TOP_K):
            pltpu.make_async_copy(tile(ys_ref, dest_ref[t * TOP_K + k]), tile(buf.at[k], t),
                                  sem).start(priority=k % 2)
        return carry

    lax.fori_loop(0, tm, issue, 0)

    x = x_ref[...]
    xb = x.astype(BF16)
    g = jnp.dot(xb, wg_ref[...], preferred_element_type=F32)
    u = jnp.dot(xb, wu_ref[...], preferred_element_type=F32)
    h = (g * jax.nn.sigmoid(g) * u).astype(BF16)
    shared = jnp.dot(h, wd_ref[...], preferred_element_type=F32)

    for k in range(TOP_K):
        pltpu.make_async_copy(ys_ref.at[pl.ds(0, tm * ROW_SUB), :], buf.at[k], sem).wait()

    half = x.shape[1] // 2
    wt = wt_ref[...]
    r_lo = jnp.zeros((tm, half), F32)
    r_hi = jnp.zeros((tm, half), F32)
    for k in range(TOP_K):
        lo, hi = _unpack_pairs(_load_tiles_as_rows(buf.at[k], tm))
        wk = wt[:, k:k + 1]
        r_lo = r_lo + lo * wk
        r_hi = r_hi + hi * wk
    routed = jnp.concatenate([r_lo, r_hi], axis=1)
    y = ALPHA * x + (routed + shared)
    o_ref[...] = _layer_norm(y, g_ref[...], b_ref[...])


def _combine(dest_flat, wt_tok, x1, ys, wg_s, wu_s, wd_s, g, b):
    T, D = x1.shape
    FF = wg_s.shape[1]
    tm = _pick(T, (256, 128))
    kern = functools.partial(_combine_kernel, tm=tm)
    return pl.pallas_call(
        kern,
        grid=(T // tm,),
        in_specs=[
            pl.BlockSpec((tm * TOP_K,), lambda i: (i,), memory_space=pltpu.SMEM),
            pl.BlockSpec((tm, TOP_K), lambda i: (i, 0)),
            pl.BlockSpec((tm, D), lambda i: (i, 0)),
            pl.BlockSpec(memory_space=pl.ANY),
            pl.BlockSpec((D, FF), lambda i: (0, 0)),
            pl.BlockSpec((D, FF), lambda i: (0, 0)),
            pl.BlockSpec((FF, D), lambda i: (0, 0)),
            pl.BlockSpec((1, D), lambda i: (0, 0)),
            pl.BlockSpec((1, D), lambda i: (0, 0)),
        ],
        out_specs=pl.BlockSpec((tm, D), lambda i: (i, 0)),
        out_shape=jax.ShapeDtypeStruct((T, D), F32),
        scratch_shapes=[pltpu.VMEM((TOP_K, tm * ROW_SUB, LANES), U32), pltpu.SemaphoreType.DMA(())],
        compiler_params=_cparams(("arbitrary",)),
        name="moe_combine_ln2",
    )(dest_flat, wt_tok, x1, ys, wg_s, wu_s, wd_s, g, b)


def _rope_tables(positions):
    half = ROT_DIM // 2
    inv_freq = ROPE_THETA ** (-jnp.arange(0, ROT_DIM, 2, dtype=F32) / ROT_DIM)
    ang = positions.reshape(-1).astype(F32)[:, None] * inv_freq
    cos = jnp.cos(ang)
    sin = jnp.sin(ang)
    T = ang.shape[0]
    pad = HEAD_DIM - ROT_DIM
    c64 = jnp.concatenate([cos, cos, jnp.ones((T, pad), F32)], axis=1)
    sa64 = jnp.concatenate([jnp.zeros((T, half), F32), sin, jnp.zeros((T, pad), F32)], axis=1)
    sb64 = jnp.concatenate([-sin, jnp.zeros((T, half + pad), F32)], axis=1)
    rep = LANES // HEAD_DIM
    return jnp.tile(c64, (1, rep)), jnp.tile(sa64, (1, rep)), jnp.tile(sb64, (1, rep))


def kernel(x, positions, w_in, lam_q1, lam_k1, lam_q2, lam_k2, subln_g, sgu_ln_g, sgu_ln_b,
           w_spatial, b_spatial, w_out, ln1_g, ln1_b, w_router, router_bias, w_gate_exp,
           w_up_exp, w_down_exp, w_gate_sh, w_up_sh, w_down_sh, ln2_g, ln2_b):
    B, S, D = x.shape
    T = B * S
    l = 0
    x2 = x.reshape(T, D)
    n_exp = w_router.shape[-1]
    blk = 256

    w_in_b = w_in[l].astype(BF16)
    w_out_b = w_out[l].astype(BF16)
    rc, rsa, rsb = _rope_tables(positions)
    lam = (jnp.exp(jnp.sum(lam_q1[l].astype(F32) * lam_k1[l].astype(F32)))
           - jnp.exp(jnp.sum(lam_q2[l].astype(F32) * lam_k2[l].astype(F32)))
           + LAMBDA_INIT).reshape(1, 1).astype(F32)
    bs_full = jnp.broadcast_to(b_spatial[l][:, :, None], (N_SG_GROUPS, SG_CHUNK, SG_DIM)).astype(F32)
    w_rt = w_router[l].astype(F32).T
    w_rt_hi = w_rt.astype(BF16)
    w_rt_lo = (w_rt - w_rt_hi.astype(F32)).astype(BF16)
    bias_col = router_bias[l].astype(F32).reshape(n_exp, 1)

    proj = _inproj(x2, w_in_b, rc, rsa, rsb)
    attn = _attention(proj, lam, subln_g[l].reshape(1, V_DIM).astype(F32), B, S)
    sg = _sgu(proj, sgu_ln_g[l].astype(F32), sgu_ln_b[l].astype(F32), w_spatial[l], bs_full)
    x1, x1p = _outproj(attn, sg, x2, w_out_b, ln1_g[l].reshape(1, D), ln1_b[l].reshape(1, D))

    eidx, wt, rank, cnt = _router(x1, w_rt_hi, w_rt_lo, bias_col)

    counts = cnt.reshape(n_exp).astype(I32)
    padded = ((counts + blk - 1) // blk) * blk
    pad_end = jnp.cumsum(padded)
    pad_off = pad_end - padded
    P = T * TOP_K + n_exp * blk
    assert P < 2 ** 24
    dest = _dest_rows(eidx, rank, pad_off.astype(F32).reshape(n_exp, 1))
    dest_flat = dest.T.reshape(T * TOP_K)
    wt_tok = wt.T
    nb = P // blk
    block_start = jnp.arange(nb, dtype=I32) * blk
    block_expert = jnp.minimum(
        jnp.sum((pad_end[None, :] <= block_start[:, None]).astype(I32), axis=1), n_exp - 1)
    nb_used = (pad_end[-1] // blk).reshape(1).astype(I32)
    live_blk = jnp.arange(nb, dtype=I32) < nb_used[0]
    fresh = jnp.concatenate([jnp.ones((1,), I32),
                             (block_expert[1:] != block_expert[:-1]).astype(I32)])
    slot = ((jnp.cumsum(fresh) - 1) % 2).astype(I32)
    later = live_blk[None, :] & (block_expert[None, :] > block_expert[:, None])
    next_expert = jnp.min(jnp.where(later, block_expert[None, :], n_exp), axis=1)
    next_expert = jnp.where(next_expert < n_exp, next_expert, -1).astype(I32)

    xs = _dispatch(counts, pad_off.astype(I32), dest_flat, x1p, P, blk)
    ys = _gmm(block_expert, nb_used, fresh, slot, next_expert, xs,
              w_gate_exp[l], w_up_exp[l], w_down_exp[l], blk)
    out = _combine(dest_flat, wt_tok, x1, ys,
                   w_gate_sh[l].astype(BF16), w_up_sh[l].astype(BF16), w_down_sh[l].astype(BF16),
                   ln2_g[l].reshape(1, D), ln2_b[l].reshape(1, D))
    return out.reshape(B, S, D)
```

```python
import functools
import math

import jax
import jax.numpy as jnp
from jax import lax
from jax.experimental import pallas as pl
from jax.experimental.pallas import tpu as pltpu

F32 = jnp.float32
BF16 = jnp.bfloat16
U32 = jnp.uint32
I32 = jnp.int32

N_HEADS = 8
HEAD_DIM = 64
V_DIM = 2 * HEAD_DIM
N_SG_GROUPS = 8
SG_DIM = 128
SG_CHUNK = 128
ROPE_THETA = 500000.0
ROT_DIM = HEAD_DIM // 4
TOP_K = 8
N_EXPERT_GROUPS = 8
TOPK_GROUPS = 4
ROUTED_SCALE = 2.5
LN_EPS = 1e-5
DEPTH = 1
ALPHA = (2 * DEPTH) ** 0.25
LAMBDA_INIT = 0.8 - 0.6 * math.exp(-0.3 * 0)
QK_SCALE = HEAD_DIM ** -0.5 * math.log2(math.e)

LANES = 128
SEG = 1024
ATTN_TQ = (512, 256, 128)
ATTN_TK = (1024, 512, 256, 128)
VMEM_LIMIT = 56 * 1024 * 1024


def _cparams(sem, vmem=VMEM_LIMIT):
    return pltpu.CompilerParams(dimension_semantics=sem, vmem_limit_bytes=vmem)


def _pick(n, prefs):
    for p in prefs:
        if n % p == 0:
            return p
    return n


def _inproj_kernel(x_ref, w_ref, c_ref, sa_ref, sb_ref, o_ref, xb_ref):
    j = pl.program_id(1)

    @pl.when(j == 0)
    def _():
        xb_ref[...] = x_ref[...].astype(BF16)

    acc = jnp.dot(xb_ref[...], w_ref[...], preferred_element_type=F32)

    @pl.when(j < 2)
    def _():
        scale = jnp.where(j == 0, QK_SCALE, 1.0).astype(F32)
        c = c_ref[...] * scale
        sa = sa_ref[...] * scale
        sb = sb_ref[...] * scale
        for blk in range(SEG // LANES):
            a = acc[:, blk * LANES:(blk + 1) * LANES]
            r = a * c + pltpu.roll(a, 8, 1) * sa + pltpu.roll(a, LANES - 8, 1) * sb
            o_ref[:, blk * LANES:(blk + 1) * LANES] = r.astype(BF16)

    @pl.when(j == 2)
    def _():
        o_ref[...] = acc.astype(BF16)

    @pl.when(j > 2)
    def _():
        g = 0.5 * acc * (1.0 + lax.erf(acc * (2.0 ** -0.5)))
        o_ref[...] = g.astype(BF16)


def _inproj(x2, w_b, rc, rsa, rsb):
    T, D = x2.shape
    n_seg = w_b.shape[1] // SEG
    tm = _pick(T, (512, 256, 128))
    return pl.pallas_call(
        _inproj_kernel,
        grid=(T // tm, n_seg),
        in_specs=[
            pl.BlockSpec((tm, D), lambda i, j: (i, 0)),
            pl.BlockSpec((D, SEG), lambda i, j: (0, j)),
            pl.BlockSpec((tm, LANES), lambda i, j: (i, 0)),
            pl.BlockSpec((tm, LANES), lambda i, j: (i, 0)),
            pl.BlockSpec((tm, LANES), lambda i, j: (i, 0)),
        ],
        out_specs=pl.BlockSpec((tm, SEG), lambda i, j: (i, j)),
        out_shape=jax.ShapeDtypeStruct((T, n_seg * SEG), BF16),
        scratch_shapes=[pltpu.VMEM((tm, D), BF16)],
        compiler_params=_cparams(("arbitrary", "arbitrary")),
        name="inproj",
    )(x2, w_b, rc, rsa, rsb)


def _attn_kernel(lam_ref, q_ref, k_ref, v_ref, g_ref, o_ref,
                 acc_a, acc_b, m_a, m_b, s_a0, s_a1, s_b0, s_b1, *, tq, tk):
    qi = pl.program_id(2)
    s_a = (s_a0, s_a1)
    s_b = (s_b0, s_b1)
    dn = (((1,), (1,)), ((), ()))
    q = q_ref[...]
    lane = lax.broadcasted_iota(I32, q.shape, 1)
    zero = jnp.zeros_like(q)
    q_a = jnp.where(lane < HEAD_DIM, q, zero)
    q_b = jnp.where(lane >= HEAD_DIM, q, zero)

    for r in (acc_a, acc_b):
        r[...] = jnp.zeros(r.shape, F32)
    for r in (m_a, m_b):
        r[...] = jnp.full(r.shape, -jnp.inf, F32)

    def scores(ki, slot):
        start = pl.multiple_of(ki * tk, tk)
        k = k_ref[pl.ds(start, tk), :]
        s_a[slot][...] = lax.dot_general(q_a, k, dn, preferred_element_type=F32)
        s_b[slot][...] = lax.dot_general(q_b, k, dn, preferred_element_type=F32)


    def consume(ki, slot, masked):
        start = pl.multiple_of(ki * tk, tk)
        v = v_ref[pl.ds(start, tk), :]
        v_ext = jnp.concatenate([v, jnp.ones_like(v)], axis=1)
        if masked:
            row = lax.broadcasted_iota(I32, (tq, tk), 0) + (qi * tq - start)
            col = lax.broadcasted_iota(I32, (tq, tk), 1)
            keep = col <= row
        for s_ref, acc, m in ((s_a[slot], acc_a, m_a), (s_b[slot], acc_b, m_b)):
            s = s_ref[...]
            if masked:
                s = jnp.where(keep, s, -jnp.inf)
            m_prev = m[...]
            m_new = jnp.maximum(m_prev, jnp.max(s, axis=1, keepdims=True))
            alpha = jnp.exp2(m_prev - m_new)
            p = jnp.exp2(s - m_new)
            acc[...] = alpha * acc[...] + jnp.dot(p.astype(BF16), v_ext, preferred_element_type=F32)
            m[...] = m_new

    n_full = (qi * tq) // tk
    n_pairs = n_full // 2
    scores(0, 0)

    def body(t, carry):
        ki = 2 * t
        scores(ki + 1, 1)
        consume(ki, 0, False)
        scores(ki + 2, 0)
        consume(ki + 1, 1, False)
        return carry

    lax.fori_loop(0, n_pairs, body, 0)

    @pl.when(n_full % 2 == 1)
    def _():
        scores(n_full, 1)
        consume(n_full - 1, 0, False)
        consume(n_full, 1, True)

    @pl.when(n_full % 2 == 0)
    def _():
        consume(n_full, 0, True)

    lam = lam_ref[0, 0]
    o = (acc_a[:, :V_DIM] / acc_a[:, V_DIM:]) - lam * (acc_b[:, :V_DIM] / acc_b[:, V_DIM:])
    ms = jnp.mean(o * o, axis=1, keepdims=True)
    o = o * lax.rsqrt(ms + LN_EPS) * g_ref[...] * (1.0 - LAMBDA_INIT)
    o_ref[...] = o.astype(BF16)


def _attention(proj, lam, subln_g, B, S):
    T = B * S
    tq = _pick(S, ATTN_TQ)
    nq = S // tq
    tk = _pick(S, ATTN_TK)
    assert tk % tq == 0
    kern = functools.partial(_attn_kernel, tq=tq, tk=tk)
    return pl.pallas_call(
        kern,
        grid=(B, N_HEADS, nq),
        in_specs=[
            pl.BlockSpec(memory_space=pltpu.SMEM),
            pl.BlockSpec((tq, V_DIM), lambda b, h, i: (b * nq + i, h)),
            pl.BlockSpec((S, V_DIM), lambda b, h, i: (b, N_HEADS + h)),
            pl.BlockSpec((S, V_DIM), lambda b, h, i: (b, 2 * N_HEADS + h)),
            pl.BlockSpec((1, V_DIM), lambda b, h, i: (0, 0)),
        ],
        out_specs=pl.BlockSpec((tq, V_DIM), lambda b, h, i: (b * nq + i, h)),
        out_shape=jax.ShapeDtypeStruct((T, N_HEADS * V_DIM), BF16),
        scratch_shapes=[
            pltpu.VMEM((tq, 2 * V_DIM), F32), pltpu.VMEM((tq, 2 * V_DIM), F32),
            pltpu.VMEM((tq, 1), F32), pltpu.VMEM((tq, 1), F32),
            pltpu.VMEM((tq, tk), F32), pltpu.VMEM((tq, tk), F32),
            pltpu.VMEM((tq, tk), F32), pltpu.VMEM((tq, tk), F32),
        ],
        compiler_params=_cparams(("arbitrary", "arbitrary", "arbitrary")),
        name="diff_attention",
    )(lam, proj, proj, proj, subln_g)


def _sgu_kernel(u_ref, v_ref, g_ref, b_ref, w_ref, bs_ref, o_ref, *, tm):
    g = pl.program_id(1)
    ln_g = g_ref[pl.ds(g, 1), :]
    ln_b = b_ref[pl.ds(g, 1), :]
    row = lax.broadcasted_iota(I32, (SG_CHUNK, SG_CHUNK), 0)
    col = lax.broadcasted_iota(I32, (SG_CHUNK, SG_CHUNK), 1)
    w = jnp.where(col <= row, w_ref[0], 0.0).astype(BF16)
    bs = bs_ref[0]
    for c in range(tm // SG_CHUNK):
        sl = slice(c * SG_CHUNK, (c + 1) * SG_CHUNK)
        v = v_ref[sl, :].astype(F32)
        mu = jnp.mean(v, axis=1, keepdims=True)
        vc = v - mu
        var = jnp.mean(vc * vc, axis=1, keepdims=True)
        vn = vc * lax.rsqrt(var + LN_EPS) * ln_g + ln_b
        mixed = jnp.dot(w, vn.astype(BF16), preferred_element_type=F32) + bs
        o_ref[sl, :] = (u_ref[sl, :].astype(F32) * mixed).astype(BF16)


def _sgu(proj, ln_g, ln_b, w_s, bs_full):
    T = proj.shape[0]
    tm = _pick(T, (1024, 512, 256, 128))
    u_blk = 3 * SEG // SG_DIM
    v_blk = 4 * SEG // SG_DIM
    kern = functools.partial(_sgu_kernel, tm=tm)
    return pl.pallas_call(
        kern,
        grid=(T // tm, N_SG_GROUPS),
        in_specs=[
            pl.BlockSpec((tm, SG_DIM), lambda i, g: (i, u_blk + g)),
            pl.BlockSpec((tm, SG_DIM), lambda i, g: (i, v_blk + g)),
            pl.BlockSpec((N_SG_GROUPS, SG_DIM), lambda i, g: (0, 0)),
            pl.BlockSpec((N_SG_GROUPS, SG_DIM), lambda i, g: (0, 0)),
            pl.BlockSpec((1, SG_CHUNK, SG_CHUNK), lambda i, g: (g, 0, 0)),
            pl.BlockSpec((1, SG_CHUNK, SG_DIM), lambda i, g: (g, 0, 0)),
        ],
        out_specs=pl.BlockSpec((tm, SG_DIM), lambda i, g: (i, g)),
        out_shape=jax.ShapeDtypeStruct((T, N_SG_GROUPS * SG_DIM), BF16),
        compiler_params=_cparams(("arbitrary", "arbitrary")),
        name="spatial_gating",
    )(proj, proj, ln_g, ln_b, w_s, bs_full)


def _pack_halves(y_lo, y_hi):
    lo = lax.bitcast_convert_type(y_lo.astype(BF16).astype(F32), U32)
    hi = lax.bitcast_convert_type(y_hi.astype(BF16).astype(F32), U32)
    return (lo >> 16) | (hi & jnp.uint32(0xFFFF0000))


def _pack_pairs(y):
    half = y.shape[1] // 2
    return _pack_halves(y[:, :half], y[:, half:])


def _unpack_pairs(p):
    lo = lax.bitcast_convert_type(p << 16, F32)
    hi = lax.bitcast_convert_type(p & jnp.uint32(0xFFFF0000), F32)
    return lo, hi


ROW_SUB = 8


def _store_rows_as_tiles(ref, packed):
    n = packed.shape[0]
    for s in range(ROW_SUB):
        ref[pl.ds(s, n, stride=ROW_SUB), :] = packed[:, s * LANES:(s + 1) * LANES]


def _load_tiles_as_rows(ref, n):
    return jnp.concatenate(
        [ref[pl.ds(s, n, stride=ROW_SUB), :] for s in range(ROW_SUB)], axis=1)


def _layer_norm(y, g, b):
    mu = jnp.mean(y, axis=1, keepdims=True)
    yc = y - mu
    var = jnp.mean(yc * yc, axis=1, keepdims=True)
    return yc * lax.rsqrt(var + LN_EPS) * g + b


def _outproj_kernel(a_ref, s_ref, x_ref, w_ref, g_ref, b_ref, x1_ref, xp_ref):
    half = w_ref.shape[0] // 2
    mix = jnp.dot(a_ref[...], w_ref[:half, :], preferred_element_type=F32)
    mix = mix + jnp.dot(s_ref[...], w_ref[half:, :], preferred_element_type=F32)
    y = ALPHA * x_ref[...] + mix
    x1 = _layer_norm(y, g_ref[...], b_ref[...])
    x1_ref[...] = x1
    _store_rows_as_tiles(xp_ref, _pack_pairs(x1))


def _outproj(attn, sg, x2, w_b, g, b):
    T, D = x2.shape
    assert D // 2 == ROW_SUB * LANES
    tm = _pick(T, (256, 128))
    wa = attn.shape[1]
    ws = sg.shape[1]
    return pl.pallas_call(
        _outproj_kernel,
        grid=(T // tm,),
        in_specs=[
            pl.BlockSpec((tm, wa), lambda i: (i, 0)),
            pl.BlockSpec((tm, ws), lambda i: (i, 0)),
            pl.BlockSpec((tm, D), lambda i: (i, 0)),
            pl.BlockSpec((wa + ws, D), lambda i: (0, 0)),
            pl.BlockSpec((1, D), lambda i: (0, 0)),
            pl.BlockSpec((1, D), lambda i: (0, 0)),
        ],
        out_specs=[
            pl.BlockSpec((tm, D), lambda i: (i, 0)),
            pl.BlockSpec((tm * ROW_SUB, LANES), lambda i: (i, 0)),
        ],
        out_shape=[
            jax.ShapeDtypeStruct((T, D), F32),
            jax.ShapeDtypeStruct((T * ROW_SUB, LANES), U32),
        ],
        compiler_params=_cparams(("arbitrary",)),
        name="outproj_ln1",
    )(attn, sg, x2, w_b, g, b)


def _router_kernel(x_ref, wh_ref, wl_ref, bias_ref, idx_ref, wt_ref, rank_ref, cnt_ref,
                   carry_ref, *, tm, n_exp):
    step = pl.program_id(0)

    @pl.when(step == 0)
    def _():
        carry_ref[...] = jnp.zeros(carry_ref.shape, F32)

    x = x_ref[...]
    xh = x.astype(BF16)
    xl = (x - xh.astype(F32)).astype(BF16)
    dn = (((1,), (1,)), ((), ()))
    wh = wh_ref[...]
    logits = (lax.dot_general(wh, xh, dn, preferred_element_type=F32)
              + lax.dot_general(wh, xl, dn, preferred_element_type=F32)
              + lax.dot_general(wl_ref[...], xh, dn, preferred_element_type=F32))
    scores = jax.nn.sigmoid(logits)
    choice = scores + bias_ref[...]

    per = n_exp // N_EXPERT_GROUPS
    neg = jnp.float32(-jnp.inf)
    sub = lax.broadcasted_iota(I32, (per, tm), 0)
    gscore = []
    for g in range(N_EXPERT_GROUPS):
        cg = choice[g * per:(g + 1) * per, :]
        m1 = jnp.max(cg, axis=0, keepdims=True)
        i1 = jnp.min(jnp.where(cg == m1, sub, per), axis=0, keepdims=True)
        m2 = jnp.max(jnp.where(sub == i1, neg, cg), axis=0, keepdims=True)
        gscore.append(m1 + m2)
    masked_parts = []
    for g in range(N_EXPERT_GROUPS):
        beat = jnp.zeros((1, tm), I32)
        for h in range(N_EXPERT_GROUPS):
            if h == g:
                continue
            wins = (gscore[h] > gscore[g]) | ((gscore[h] == gscore[g]) & (h < g))
            beat = beat + wins.astype(I32)
        keep = beat < TOPK_GROUPS
        cg = choice[g * per:(g + 1) * per, :]
        masked_parts.append(jnp.where(keep, cg, neg))
    masked = jnp.concatenate(masked_parts, axis=0)

    eidx = lax.broadcasted_iota(I32, (n_exp, tm), 0)
    hot = jnp.zeros((n_exp, tm), F32)
    sel_idx, sel_w, sel_hot = [], [], []
    for _ in range(TOP_K):
        mk = jnp.max(masked, axis=0, keepdims=True)
        ik = jnp.min(jnp.where(masked == mk, eidx, n_exp), axis=0, keepdims=True)
        one = eidx == ik
        sel_idx.append(ik)
        sel_w.append(jnp.sum(jnp.where(one, scores, 0.0), axis=0, keepdims=True))
        sel_hot.append(one)
        hot = hot + one.astype(F32)
        masked = jnp.where(one, neg, masked)

    wsum = sel_w[0]
    for k in range(1, TOP_K):
        wsum = wsum + sel_w[k]

    r = lax.broadcasted_iota(I32, (tm, tm), 0)
    c = lax.broadcasted_iota(I32, (tm, tm), 1)
    upper = (r < c).astype(BF16)
    pref = jnp.dot(hot.astype(BF16), upper, preferred_element_type=F32) + carry_ref[...]
    for k in range(TOP_K):
        idx_ref[k:k + 1, :] = sel_idx[k]
        wt_ref[k:k + 1, :] = sel_w[k] / wsum * ROUTED_SCALE
        rk = jnp.sum(jnp.where(sel_hot[k], pref, 0.0), axis=0, keepdims=True)
        rank_ref[k:k + 1, :] = rk.astype(I32)
    carry_ref[...] = carry_ref[...] + jnp.sum(hot, axis=1, keepdims=True)
    cnt_ref[...] = carry_ref[...]


def _router(x1, wh, wl, bias_col):
    T, D = x1.shape
    n_exp = wh.shape[0]
    tm = _pick(T, (512, 256, 128))
    kern = functools.partial(_router_kernel, tm=tm, n_exp=n_exp)
    return pl.pallas_call(
        kern,
        grid=(T // tm,),
        in_specs=[
            pl.BlockSpec((tm, D), lambda i: (i, 0)),
            pl.BlockSpec((n_exp, D), lambda i: (0, 0)),
            pl.BlockSpec((n_exp, D), lambda i: (0, 0)),
            pl.BlockSpec((n_exp, 1), lambda i: (0, 0)),
        ],
        out_specs=[
            pl.BlockSpec((TOP_K, tm), lambda i: (0, i)),
            pl.BlockSpec((TOP_K, tm), lambda i: (0, i)),
            pl.BlockSpec((TOP_K, tm), lambda i: (0, i)),
            pl.BlockSpec((n_exp, 1), lambda i: (0, 0)),
        ],
        out_shape=[
            jax.ShapeDtypeStruct((TOP_K, T), I32),
            jax.ShapeDtypeStruct((TOP_K, T), F32),
            jax.ShapeDtypeStruct((TOP_K, T), I32),
            jax.ShapeDtypeStruct((n_exp, 1), F32),
        ],
        scratch_shapes=[pltpu.VMEM((n_exp, 1), F32)],
        compiler_params=_cparams(("arbitrary",)),
        name="router_topk",
    )(x1, wh, wl, bias_col)


def _dest_kernel(idx_ref, rank_ref, off_ref, o_ref, *, tm, n_exp):
    eidx = lax.broadcasted_iota(I32, (n_exp, tm), 0)
    off = off_ref[...]
    for k in range(TOP_K):
        hit = eidx == idx_ref[k:k + 1, :]
        base = jnp.sum(jnp.where(hit, off, 0.0), axis=0, keepdims=True)
        o_ref[k:k + 1, :] = base.astype(I32) + rank_ref[k:k + 1, :]


def _dest_rows(eidx, rank, pad_off_col):
    K, T = eidx.shape
    n_exp = pad_off_col.shape[0]
    tm = _pick(T, (1024, 512, 256, 128))
    kern = functools.partial(_dest_kernel, tm=tm, n_exp=n_exp)
    return pl.pallas_call(
        kern,
        grid=(T // tm,),
        in_specs=[
            pl.BlockSpec((K, tm), lambda i: (0, i)),
            pl.BlockSpec((K, tm), lambda i: (0, i)),
            pl.BlockSpec((n_exp, 1), lambda i: (0, 0)),
        ],
        out_specs=pl.BlockSpec((K, tm), lambda i: (0, i)),
        out_shape=jax.ShapeDtypeStruct((K, T), I32),
        compiler_params=_cparams(("arbitrary",)),
        name="moe_dest_rows",
    )(eidx, rank, pad_off_col)


def _dispatch_kernel(cnt_ref, off_ref, nbu_ref, dest_ref, x_ref, xs_ref, zero_ref, sem, *,
                     tm, blk, n_exp, exp_per_step, n_blocks, tail_per_step):
    step = pl.program_id(0)

    def tile(ref, r):
        return ref.at[pl.ds(pl.multiple_of(r * ROW_SUB, ROW_SUB), ROW_SUB), :]

    def issue(t, carry):
        src = tile(x_ref, t)
        for k in range(TOP_K):
            pltpu.make_async_copy(src, tile(xs_ref, dest_ref[t * TOP_K + k]), sem).start(priority=k % 2)
        return carry

    lax.fori_loop(0, tm, issue, 0)

    zero_ref[...] = jnp.zeros(zero_ref.shape, U32)

    def zero_copy(d):
        return pltpu.make_async_copy(tile(zero_ref, 0), tile(xs_ref, d), sem)

    def zero_block(bi):
        rows = pl.ds(pl.multiple_of(bi * (blk * ROW_SUB), blk * ROW_SUB), blk * ROW_SUB)
        return pltpu.make_async_copy(zero_ref, xs_ref.at[rows, :], sem)

    tail_blocks = []
    for j in range(tail_per_step):
        bi = nbu_ref[0] + step * tail_per_step + j
        tail_blocks.append(bi)

        @pl.when(bi < n_blocks)
        def _():
            zero_block(bi).start()

    n_pad_total = jnp.int32(0)
    for j in range(exp_per_step):
        e = jnp.minimum(step * exp_per_step + j, n_exp - 1)
        valid = (step * exp_per_step + j) < n_exp
        cnt = cnt_ref[e]
        n_pad = jnp.where(valid, (blk - cnt % blk) % blk, 0)
        base = off_ref[e] + cnt

        def zissue(i, carry, base=base):
            zero_copy(base + i).start()
            return carry

        lax.fori_loop(0, n_pad, zissue, 0)
        n_pad_total = n_pad_total + n_pad

    for k in range(TOP_K):
        pltpu.make_async_copy(x_ref, xs_ref.at[pl.ds(0, tm * ROW_SUB), :], sem).wait()

    def wait(i, carry):
        zero_copy(0).wait()
        return carry

    lax.fori_loop(0, n_pad_total, wait, 0)

    for bi in tail_blocks:
        @pl.when(bi < n_blocks)
        def _():
            zero_block(0).wait()


def _dispatch(counts, pad_off, nb_used, dest_flat, x1p, n_rows_out, blk):
    T = x1p.shape[0] // ROW_SUB
    n_exp = counts.shape[0]
    tm = _pick(T, (256, 128))
    n_steps = T // tm
    exp_per_step = -(-n_exp // n_steps)
    n_blocks = n_rows_out // blk
    max_tail = n_blocks - (T * TOP_K) // blk
    tail_per_step = -(-max_tail // n_steps)
    kern = functools.partial(_dispatch_kernel, tm=tm, blk=blk, n_exp=n_exp,
                             exp_per_step=exp_per_step, n_blocks=n_blocks,
                             tail_per_step=tail_per_step)
    grid_spec = pltpu.PrefetchScalarGridSpec(
        num_scalar_prefetch=3,
        grid=(n_steps,),
        in_specs=[
            pl.BlockSpec((tm * TOP_K,), lambda i, c, o, n: (i,), memory_space=pltpu.SMEM),
            pl.BlockSpec((tm * ROW_SUB, LANES), lambda i, c, o, n: (i, 0)),
        ],
        out_specs=pl.BlockSpec(memory_space=pl.ANY),
        scratch_shapes=[pltpu.VMEM((blk * ROW_SUB, LANES), U32), pltpu.SemaphoreType.DMA(())],
    )
    return pl.pallas_call(
        kern,
        grid_spec=grid_spec,
        out_shape=jax.ShapeDtypeStruct((n_rows_out * ROW_SUB, LANES), U32),
        compiler_params=_cparams(("arbitrary",)),
        name="moe_dispatch",
    )(counts, pad_off, nb_used, dest_flat, x1p)


def _gmm_kernel(be_ref, nb_ref, fresh_ref, slot_ref, next_ref,
                xs_ref, wg_ref, wu_ref, wd_ref, ys_ref,
                wg_f, wu_f, wd_f, wg_b, wu_b, wd_b, sem):
    b = pl.program_id(0)
    live = b < nb_ref[0]

    def fetch(e, s):
        return (pltpu.make_async_copy(wg_ref.at[e], wg_f.at[s], sem.at[s]),
                pltpu.make_async_copy(wu_ref.at[e], wu_f.at[s], sem.at[s]),
                pltpu.make_async_copy(wd_ref.at[e], wd_f.at[s], sem.at[s]))

    @pl.when(b == 0)
    def _():
        for c in fetch(be_ref[0], 0):
            c.start()

    @pl.when(live & (fresh_ref[b] == 1))
    def _():
        s = slot_ref[b]
        for c in fetch(be_ref[b], s):
            c.wait()
        nxt = next_ref[b]

        @pl.when(nxt >= 0)
        def _():
            for c in fetch(nxt, 1 - s):
                c.start()

        wg_b[...] = wg_f[s].astype(BF16)
        wu_b[...] = wu_f[s].astype(BF16)
        wd_b[...] = wd_f[s].astype(BF16)

    @pl.when(live)
    def _():
        blk = xs_ref.shape[0] // ROW_SUB
        lo, hi = _unpack_pairs(_load_tiles_as_rows(xs_ref, blk))
        lo = lo.astype(BF16)
        hi = hi.astype(BF16)
        half = lo.shape[1]
        g = (jnp.dot(lo, wg_b[:half, :], preferred_element_type=F32)
             + jnp.dot(hi, wg_b[half:, :], preferred_element_type=F32))
        u = (jnp.dot(lo, wu_b[:half, :], preferred_element_type=F32)
             + jnp.dot(hi, wu_b[half:, :], preferred_element_type=F32))
        h = (g * jax.nn.sigmoid(g) * u).astype(BF16)
        y_lo = jnp.dot(h, wd_b[:, :half], preferred_element_type=F32)
        y_hi = jnp.dot(h, wd_b[:, half:], preferred_element_type=F32)
        _store_rows_as_tiles(ys_ref, _pack_halves(y_lo, y_hi))

    @pl.when(jnp.logical_not(live))
    def _():
        ys_ref[...] = jnp.zeros(ys_ref.shape, U32)


def _gmm(block_expert, nb_used, fresh, slot, next_expert, xs, wg, wu, wd, blk):
    P = xs.shape[0] // ROW_SUB
    n_exp, D, FF = wg.shape
    nb = P // blk

    def row_map(b, be, nbu, fr, sl, nx):
        return (jnp.minimum(b, nbu[0] - 1), 0)

    grid_spec = pltpu.PrefetchScalarGridSpec(
        num_scalar_prefetch=5,
        grid=(nb,),
        in_specs=[
            pl.BlockSpec((blk * ROW_SUB, LANES), row_map),
            pl.BlockSpec(memory_space=pl.ANY),
            pl.BlockSpec(memory_space=pl.ANY),
            pl.BlockSpec(memory_space=pl.ANY),
        ],
        out_specs=pl.BlockSpec((blk * ROW_SUB, LANES), lambda b, be, nbu, fr, sl, nx: (b, 0)),
        scratch_shapes=[
            pltpu.VMEM((2, D, FF), F32), pltpu.VMEM((2, D, FF), F32), pltpu.VMEM((2, FF, D), F32),
            pltpu.VMEM((D, FF), BF16), pltpu.VMEM((D, FF), BF16), pltpu.VMEM((FF, D), BF16),
            pltpu.SemaphoreType.DMA((2,)),
        ],
    )
    return pl.pallas_call(
        _gmm_kernel,
        grid_spec=grid_spec,
        out_shape=jax.ShapeDtypeStruct((P * ROW_SUB, LANES), U32),
        compiler_params=_cparams(("arbitrary",)),
        name="moe_grouped_swiglu",
    )(block_expert, nb_used, fresh, slot, next_expert, xs, wg, wu, wd)


def _combine_kernel(dest_ref, dest_next_ref, wt_ref, x_ref, ys_ref, wg_ref, wu_ref, wd_ref,
                    g_ref, b_ref, o_ref, buf, sem, *, tm):
    step = pl.program_id(0)
    n_steps = pl.num_programs(0)
    slot = step % 2

    def tile(ref, r):
        return ref.at[pl.ds(pl.multiple_of(r * ROW_SUB, ROW_SUB), ROW_SUB), :]

    def gather(idx_ref, s):
        def issue(t, carry):
            for k in range(TOP_K):
                pltpu.make_async_copy(tile(ys_ref, idx_ref[t * TOP_K + k]), tile(buf.at[s, k], t),
                                      sem.at[s]).start(priority=k % 2)
            return carry

        lax.fori_loop(0, tm, issue, 0)

    @pl.when(step == 0)
    def _():
        gather(dest_ref, 0)

    @pl.when(step + 1 < n_steps)
    def _():
        gather(dest_next_ref, 1 - slot)

    for k in range(TOP_K):
        pltpu.make_async_copy(ys_ref.at[pl.ds(0, tm * ROW_SUB), :], buf.at[slot, k],
                              sem.at[slot]).wait()

    x = x_ref[...]
    xb = x.astype(BF16)
    g = jnp.dot(xb, wg_ref[...], preferred_element_type=F32)
    u = jnp.dot(xb, wu_ref[...], preferred_element_type=F32)
    h = (g * jax.nn.sigmoid(g) * u).astype(BF16)
    shared = jnp.dot(h, wd_ref[...], preferred_element_type=F32)

    half = x.shape[1] // 2
    wt = wt_ref[...]
    r_lo = jnp.zeros((tm, half), F32)
    r_hi = jnp.zeros((tm, half), F32)
    for k in range(TOP_K):
        lo, hi = _unpack_pairs(_load_tiles_as_rows(buf.at[slot, k], tm))
        wk = wt[:, k:k + 1]
        r_lo = r_lo + lo * wk
        r_hi = r_hi + hi * wk
    routed = jnp.concatenate([r_lo, r_hi], axis=1)
    y = ALPHA * x + (routed + shared)
    o_ref[...] = _layer_norm(y, g_ref[...], b_ref[...])


def _combine(dest_flat, wt_tok, x1, ys, wg_s, wu_s, wd_s, g, b):
    T, D = x1.shape
    FF = wg_s.shape[1]
    tm = _pick(T, (256, 128))
    n_steps = T // tm
    kern = functools.partial(_combine_kernel, tm=tm)
    return pl.pallas_call(
        kern,
        grid=(n_steps,),
        in_specs=[
            pl.BlockSpec((tm * TOP_K,), lambda i: (i,), memory_space=pltpu.SMEM),
            pl.BlockSpec((tm * TOP_K,), lambda i: (jnp.minimum(i + 1, n_steps - 1),),
                         memory_space=pltpu.SMEM),
            pl.BlockSpec((tm, TOP_K), lambda i: (i, 0)),
            pl.BlockSpec((tm, D), lambda i: (i, 0)),
            pl.BlockSpec(memory_space=pl.ANY),
            pl.BlockSpec((D, FF), lambda i: (0, 0)),
            pl.BlockSpec((D, FF), lambda i: (0, 0)),
            pl.BlockSpec((FF, D), lambda i: (0, 0)),
            pl.BlockSpec((1, D), lambda i: (0, 0)),
            pl.BlockSpec((1, D), lambda i: (0, 0)),
        ],
        out_specs=pl.BlockSpec((tm, D), lambda i: (i, 0)),
        out_shape=jax.ShapeDtypeStruct((T, D), F32),
        scratch_shapes=[pltpu.VMEM((2, TOP_K, tm * ROW_SUB, LANES), U32),
                        pltpu.SemaphoreType.DMA((2,))],
        compiler_params=_cparams(("arbitrary",)),
        name="moe_combine_ln2",
    )(dest_flat, dest_flat, wt_tok, x1, ys, wg_s, wu_s, wd_s, g, b)


def _rope_tables(positions):
    half = ROT_DIM // 2
    inv_freq = ROPE_THETA ** (-jnp.arange(0, ROT_DIM, 2, dtype=F32) / ROT_DIM)
    ang = positions.reshape(-1).astype(F32)[:, None] * inv_freq
    cos = jnp.cos(ang)
    sin = jnp.sin(ang)
    T = ang.shape[0]
    pad = HEAD_DIM - ROT_DIM
    c64 = jnp.concatenate([cos, cos, jnp.ones((T, pad), F32)], axis=1)
    sa64 = jnp.concatenate([jnp.zeros((T, half), F32), sin, jnp.zeros((T, pad), F32)], axis=1)
    sb64 = jnp.concatenate([-sin, jnp.zeros((T, half + pad), F32)], axis=1)
    rep = LANES // HEAD_DIM
    return jnp.tile(c64, (1, rep)), jnp.tile(sa64, (1, rep)), jnp.tile(sb64, (1, rep))


def kernel(x, positions, w_in, lam_q1, lam_k1, lam_q2, lam_k2, subln_g, sgu_ln_g, sgu_ln_b,
           w_spatial, b_spatial, w_out, ln1_g, ln1_b, w_router, router_bias, w_gate_exp,
           w_up_exp, w_down_exp, w_gate_sh, w_up_sh, w_down_sh, ln2_g, ln2_b):
    B, S, D = x.shape
    T = B * S
    l = 0
    x2 = x.reshape(T, D)
    n_exp = w_router.shape[-1]
    blk = 256

    w_in_b = w_in[l].astype(BF16)
    w_out_b = w_out[l].astype(BF16)
    rc, rsa, rsb = _rope_tables(positions)
    lam = (jnp.exp(jnp.sum(lam_q1[l].astype(F32) * lam_k1[l].astype(F32)))
           - jnp.exp(jnp.sum(lam_q2[l].astype(F32) * lam_k2[l].astype(F32)))
           + LAMBDA_INIT).reshape(1, 1).astype(F32)
    bs_full = jnp.broadcast_to(b_spatial[l][:, :, None], (N_SG_GROUPS, SG_CHUNK, SG_DIM)).astype(F32)
    w_rt = w_router[l].astype(F32).T
    w_rt_hi = w_rt.astype(BF16)
    w_rt_lo = (w_rt - w_rt_hi.astype(F32)).astype(BF16)
    bias_col = router_bias[l].astype(F32).reshape(n_exp, 1)

    proj = _inproj(x2, w_in_b, rc, rsa, rsb)
    attn = _attention(proj, lam, subln_g[l].reshape(1, V_DIM).astype(F32), B, S)
    sg = _sgu(proj, sgu_ln_g[l].astype(F32), sgu_ln_b[l].astype(F32), w_spatial[l], bs_full)
    x1, x1p = _outproj(attn, sg, x2, w_out_b, ln1_g[l].reshape(1, D), ln1_b[l].reshape(1, D))

    eidx, wt, rank, cnt = _router(x1, w_rt_hi, w_rt_lo, bias_col)

    counts = cnt.reshape(n_exp).astype(I32)
    padded = ((counts + blk - 1) // blk) * blk
    pad_end = jnp.cumsum(padded)
    pad_off = pad_end - padded
    P = T * TOP_K + n_exp * blk
    assert P < 2 ** 24
    dest = _dest_rows(eidx, rank, pad_off.astype(F32).reshape(n_exp, 1))
    dest_flat = dest.T.reshape(T * TOP_K)
    wt_tok = wt.T
    nb = P // blk
    block_start = jnp.arange(nb, dtype=I32) * blk
    block_expert = jnp.minimum(
        jnp.sum((pad_end[None, :] <= block_start[:, None]).astype(I32), axis=1), n_exp - 1)
    nb_used = (pad_end[-1] // blk).reshape(1).astype(I32)
    live_blk = jnp.arange(nb, dtype=I32) < nb_used[0]
    fresh = jnp.concatenate([jnp.ones((1,), I32),
                             (block_expert[1:] != block_expert[:-1]).astype(I32)])
    slot = ((jnp.cumsum(fresh) - 1) % 2).astype(I32)
    later = live_blk[None, :] & (block_expert[None, :] > block_expert[:, None])
    next_expert = jnp.min(jnp.where(later, block_expert[None, :], n_exp), axis=1)
    next_expert = jnp.where(next_expert < n_exp, next_expert, -1).astype(I32)

    xs = _dispatch(counts, pad_off.astype(I32), nb_used, dest_flat, x1p, P, blk)
    ys = _gmm(block_expert, nb_used, fresh, slot, next_expert, xs,
              w_gate_exp[l], w_up_exp[l], w_down_exp[l], blk)
    out = _combine(dest_flat, wt_tok, x1, ys,
                   w_gate_sh[l].astype(BF16), w_up_sh[l].astype(BF16), w_down_sh[l].astype(BF16),
                   ln2_g[l].reshape(1, D), ln2_b[l].reshape(1, D))
    return out.reshape(B, S, D)
```

```python
import functools
import math

import jax
import jax.numpy as jnp
from jax import lax
from jax.experimental import pallas as pl
from jax.experimental.pallas import tpu as pltpu

F32 = jnp.float32
BF16 = jnp.bfloat16
U32 = jnp.uint32
I32 = jnp.int32

N_HEADS = 8
HEAD_DIM = 64
V_DIM = 2 * HEAD_DIM
N_SG_GROUPS = 8
SG_DIM = 128
SG_CHUNK = 128
ROPE_THETA = 500000.0
ROT_DIM = HEAD_DIM // 4
TOP_K = 8
N_EXPERT_GROUPS = 8
TOPK_GROUPS = 4
ROUTED_SCALE = 2.5
LN_EPS = 1e-5
DEPTH = 1
ALPHA = (2 * DEPTH) ** 0.25
LAMBDA_INIT = 0.8 - 0.6 * math.exp(-0.3 * 0)
QK_SCALE = HEAD_DIM ** -0.5 * math.log2(math.e)

LANES = 128
SEG = 1024
ATTN_TQ = (512, 256, 128)
ATTN_TK = (1024, 512, 256, 128)
VMEM_LIMIT = 56 * 1024 * 1024


def _cparams(sem, vmem=VMEM_LIMIT):
    return pltpu.CompilerParams(dimension_semantics=sem, vmem_limit_bytes=vmem)


def _pick(n, prefs):
    for p in prefs:
        if n % p == 0:
            return p
    return n


def _inproj_kernel(x_ref, w_ref, c_ref, sa_ref, sb_ref, o_ref, xb_ref):
    j = pl.program_id(1)

    @pl.when(j == 0)
    def _():
        xb_ref[...] = x_ref[...].astype(BF16)

    acc = jnp.dot(xb_ref[...], w_ref[...], preferred_element_type=F32)

    @pl.when(j < 2)
    def _():
        scale = jnp.where(j == 0, QK_SCALE, 1.0).astype(F32)
        c = c_ref[...] * scale
        sa = sa_ref[...] * scale
        sb = sb_ref[...] * scale
        for blk in range(SEG // LANES):
            a = acc[:, blk * LANES:(blk + 1) * LANES]
            r = a * c + pltpu.roll(a, 8, 1) * sa + pltpu.roll(a, LANES - 8, 1) * sb
            o_ref[:, blk * LANES:(blk + 1) * LANES] = r.astype(BF16)

    @pl.when(j == 2)
    def _():
        o_ref[...] = acc.astype(BF16)

    @pl.when(j > 2)
    def _():
        g = 0.5 * acc * (1.0 + lax.erf(acc * (2.0 ** -0.5)))
        o_ref[...] = g.astype(BF16)


def _inproj(x2, w_b, rc, rsa, rsb):
    T, D = x2.shape
    n_seg = w_b.shape[1] // SEG
    tm = _pick(T, (1024, 512, 256, 128))
    return pl.pallas_call(
        _inproj_kernel,
        grid=(T // tm, n_seg),
        in_specs=[
            pl.BlockSpec((tm, D), lambda i, j: (i, 0)),
            pl.BlockSpec((D, SEG), lambda i, j: (0, j)),
            pl.BlockSpec((tm, LANES), lambda i, j: (i, 0)),
            pl.BlockSpec((tm, LANES), lambda i, j: (i, 0)),
            pl.BlockSpec((tm, LANES), lambda i, j: (i, 0)),
        ],
        out_specs=pl.BlockSpec((tm, SEG), lambda i, j: (i, j)),
        out_shape=jax.ShapeDtypeStruct((T, n_seg * SEG), BF16),
        scratch_shapes=[pltpu.VMEM((tm, D), BF16)],
        compiler_params=_cparams(("arbitrary", "arbitrary")),
        name="inproj",
    )(x2, w_b, rc, rsa, rsb)


def _attn_kernel(lam_ref, q_ref, k_ref, v_ref, g_ref, o_ref,
                 acc_a, acc_b, m_a, m_b, s_a0, s_a1, s_b0, s_b1, *, tq, tk):
    qi = pl.program_id(2)
    s_a = (s_a0, s_a1)
    s_b = (s_b0, s_b1)
    dn = (((1,), (1,)), ((), ()))
    q = q_ref[...]
    lane = lax.broadcasted_iota(I32, q.shape, 1)
    zero = jnp.zeros_like(q)
    q_a = jnp.where(lane < HEAD_DIM, q, zero)
    q_b = jnp.where(lane >= HEAD_DIM, q, zero)

    for r in (acc_a, acc_b):
        r[...] = jnp.zeros(r.shape, F32)
    for r in (m_a, m_b):
        r[...] = jnp.full(r.shape, -jnp.inf, F32)

    def scores(ki, slot):
        start = pl.multiple_of(ki * tk, tk)
        k = k_ref[pl.ds(start, tk), :]
        s_a[slot][...] = lax.dot_general(q_a, k, dn, preferred_element_type=F32)
        s_b[slot][...] = lax.dot_general(q_b, k, dn, preferred_element_type=F32)


    def consume(ki, slot, masked):
        start = pl.multiple_of(ki * tk, tk)
        v = v_ref[pl.ds(start, tk), :]
        v_ext = jnp.concatenate([v, jnp.ones_like(v)], axis=1)
        if masked:
            row = lax.broadcasted_iota(I32, (tq, tk), 0) + (qi * tq - start)
            col = lax.broadcasted_iota(I32, (tq, tk), 1)
            keep = col <= row
        for s_ref, acc, m in ((s_a[slot], acc_a, m_a), (s_b[slot], acc_b, m_b)):
            s = s_ref[...]
            if masked:
                s = jnp.where(keep, s, -jnp.inf)
            m_prev = m[...]
            m_new = jnp.maximum(m_prev, jnp.max(s, axis=1, keepdims=True))
            alpha = jnp.exp2(m_prev - m_new)
            p = jnp.exp2(s - m_new)
            acc[...] = alpha * acc[...] + jnp.dot(p.astype(BF16), v_ext, preferred_element_type=F32)
            m[...] = m_new

    n_full = (qi * tq) // tk
    n_pairs = n_full // 2
    scores(0, 0)

    def body(t, carry):
        ki = 2 * t
        scores(ki + 1, 1)
        consume(ki, 0, False)
        scores(ki + 2, 0)
        consume(ki + 1, 1, False)
        return carry

    lax.fori_loop(0, n_pairs, body, 0)

    @pl.when(n_full % 2 == 1)
    def _():
        scores(n_full, 1)
        consume(n_full - 1, 0, False)
        consume(n_full, 1, True)

    @pl.when(n_full % 2 == 0)
    def _():
        consume(n_full, 0, True)

    lam = lam_ref[0, 0]
    o = (acc_a[:, :V_DIM] / acc_a[:, V_DIM:]) - lam * (acc_b[:, :V_DIM] / acc_b[:, V_DIM:])
    ms = jnp.mean(o * o, axis=1, keepdims=True)
    o = o * lax.rsqrt(ms + LN_EPS) * g_ref[...] * (1.0 - LAMBDA_INIT)
    o_ref[...] = o.astype(BF16)


def _attention(proj, lam, subln_g, B, S):
    T = B * S
    tq = _pick(S, ATTN_TQ)
    nq = S // tq
    tk = _pick(S, ATTN_TK)
    assert tk % tq == 0
    kern = functools.partial(_attn_kernel, tq=tq, tk=tk)
    return pl.pallas_call(
        kern,
        grid=(B, N_HEADS, nq),
        in_specs=[
            pl.BlockSpec(memory_space=pltpu.SMEM),
            pl.BlockSpec((tq, V_DIM), lambda b, h, i: (b * nq + i, h)),
            pl.BlockSpec((S, V_DIM), lambda b, h, i: (b, N_HEADS + h)),
            pl.BlockSpec((S, V_DIM), lambda b, h, i: (b, 2 * N_HEADS + h)),
            pl.BlockSpec((1, V_DIM), lambda b, h, i: (0, 0)),
        ],
        out_specs=pl.BlockSpec((tq, V_DIM), lambda b, h, i: (b * nq + i, h)),
        out_shape=jax.ShapeDtypeStruct((T, N_HEADS * V_DIM), BF16),
        scratch_shapes=[
            pltpu.VMEM((tq, 2 * V_DIM), F32), pltpu.VMEM((tq, 2 * V_DIM), F32),
            pltpu.VMEM((tq, 1), F32), pltpu.VMEM((tq, 1), F32),
            pltpu.VMEM((tq, tk), F32), pltpu.VMEM((tq, tk), F32),
            pltpu.VMEM((tq, tk), F32), pltpu.VMEM((tq, tk), F32),
        ],
        compiler_params=_cparams(("arbitrary", "arbitrary", "arbitrary")),
        name="diff_attention",
    )(lam, proj, proj, proj, subln_g)


def _sgu_kernel(u_ref, v_ref, g_ref, b_ref, w_ref, bs_ref, o_ref, *, tm):
    g = pl.program_id(1)
    ln_g = g_ref[pl.ds(g, 1), :]
    ln_b = b_ref[pl.ds(g, 1), :]
    row = lax.broadcasted_iota(I32, (SG_CHUNK, SG_CHUNK), 0)
    col = lax.broadcasted_iota(I32, (SG_CHUNK, SG_CHUNK), 1)
    w = jnp.where(col <= row, w_ref[0], 0.0).astype(BF16)
    bs = bs_ref[0]
    for c in range(tm // SG_CHUNK):
        sl = slice(c * SG_CHUNK, (c + 1) * SG_CHUNK)
        v = v_ref[sl, :].astype(F32)
        mu = jnp.mean(v, axis=1, keepdims=True)
        vc = v - mu
        var = jnp.mean(vc * vc, axis=1, keepdims=True)
        vn = vc * lax.rsqrt(var + LN_EPS) * ln_g + ln_b
        mixed = jnp.dot(w, vn.astype(BF16), preferred_element_type=F32) + bs
        o_ref[sl, :] = (u_ref[sl, :].astype(F32) * mixed).astype(BF16)


def _sgu(proj, ln_g, ln_b, w_s, bs_full):
    T = proj.shape[0]
    tm = _pick(T, (1024, 512, 256, 128))
    u_blk = 3 * SEG // SG_DIM
    v_blk = 4 * SEG // SG_DIM
    kern = functools.partial(_sgu_kernel, tm=tm)
    return pl.pallas_call(
        kern,
        grid=(T // tm, N_SG_GROUPS),
        in_specs=[
            pl.BlockSpec((tm, SG_DIM), lambda i, g: (i, u_blk + g)),
            pl.BlockSpec((tm, SG_DIM), lambda i, g: (i, v_blk + g)),
            pl.BlockSpec((N_SG_GROUPS, SG_DIM), lambda i, g: (0, 0)),
            pl.BlockSpec((N_SG_GROUPS, SG_DIM), lambda i, g: (0, 0)),
            pl.BlockSpec((1, SG_CHUNK, SG_CHUNK), lambda i, g: (g, 0, 0)),
            pl.BlockSpec((1, SG_CHUNK, SG_DIM), lambda i, g: (g, 0, 0)),
        ],
        out_specs=pl.BlockSpec((tm, SG_DIM), lambda i, g: (i, g)),
        out_shape=jax.ShapeDtypeStruct((T, N_SG_GROUPS * SG_DIM), BF16),
        compiler_params=_cparams(("arbitrary", "arbitrary")),
        name="spatial_gating",
    )(proj, proj, ln_g, ln_b, w_s, bs_full)


def _pack_halves(y_lo, y_hi):
    lo = lax.bitcast_convert_type(y_lo.astype(BF16).astype(F32), U32)
    hi = lax.bitcast_convert_type(y_hi.astype(BF16).astype(F32), U32)
    return (lo >> 16) | (hi & jnp.uint32(0xFFFF0000))


def _pack_pairs(y):
    half = y.shape[1] // 2
    return _pack_halves(y[:, :half], y[:, half:])


def _unpack_pairs(p):
    lo = lax.bitcast_convert_type(p << 16, F32)
    hi = lax.bitcast_convert_type(p & jnp.uint32(0xFFFF0000), F32)
    return lo, hi


ROW_SUB = 8


def _store_rows_as_tiles(ref, packed):
    n = packed.shape[0]
    for s in range(ROW_SUB):
        ref[pl.ds(s, n, stride=ROW_SUB), :] = packed[:, s * LANES:(s + 1) * LANES]


def _load_tiles_as_rows(ref, n):
    return jnp.concatenate(
        [ref[pl.ds(s, n, stride=ROW_SUB), :] for s in range(ROW_SUB)], axis=1)


def _layer_norm(y, g, b):
    mu = jnp.mean(y, axis=1, keepdims=True)
    yc = y - mu
    var = jnp.mean(yc * yc, axis=1, keepdims=True)
    return yc * lax.rsqrt(var + LN_EPS) * g + b


def _outproj_kernel(a_ref, s_ref, x_ref, w_ref, g_ref, b_ref, x1_ref, xp_ref):
    half = w_ref.shape[0] // 2
    mix = jnp.dot(a_ref[...], w_ref[:half, :], preferred_element_type=F32)
    mix = mix + jnp.dot(s_ref[...], w_ref[half:, :], preferred_element_type=F32)
    y = ALPHA * x_ref[...] + mix
    x1 = _layer_norm(y, g_ref[...], b_ref[...])
    x1_ref[...] = x1
    _store_rows_as_tiles(xp_ref, _pack_pairs(x1))


def _outproj(attn, sg, x2, w_b, g, b):
    T, D = x2.shape
    assert D // 2 == ROW_SUB * LANES
    tm = _pick(T, (256, 128))
    wa = attn.shape[1]
    ws = sg.shape[1]
    return pl.pallas_call(
        _outproj_kernel,
        grid=(T // tm,),
        in_specs=[
            pl.BlockSpec((tm, wa), lambda i: (i, 0)),
            pl.BlockSpec((tm, ws), lambda i: (i, 0)),
            pl.BlockSpec((tm, D), lambda i: (i, 0)),
            pl.BlockSpec((wa + ws, D), lambda i: (0, 0)),
            pl.BlockSpec((1, D), lambda i: (0, 0)),
            pl.BlockSpec((1, D), lambda i: (0, 0)),
        ],
        out_specs=[
            pl.BlockSpec((tm, D), lambda i: (i, 0)),
            pl.BlockSpec((tm * ROW_SUB, LANES), lambda i: (i, 0)),
        ],
        out_shape=[
            jax.ShapeDtypeStruct((T, D), F32),
            jax.ShapeDtypeStruct((T * ROW_SUB, LANES), U32),
        ],
        compiler_params=_cparams(("arbitrary",)),
        name="outproj_ln1",
    )(attn, sg, x2, w_b, g, b)


def _router_kernel(x_ref, wh_ref, wl_ref, bias_ref, idx_ref, wt_ref, rank_ref, cnt_ref,
                   carry_ref, *, tm, n_exp):
    step = pl.program_id(0)

    @pl.when(step == 0)
    def _():
        carry_ref[...] = jnp.zeros(carry_ref.shape, F32)

    x = x_ref[...]
    xh = x.astype(BF16)
    xl = (x - xh.astype(F32)).astype(BF16)
    dn = (((1,), (1,)), ((), ()))
    wh = wh_ref[...]
    logits = (lax.dot_general(wh, xh, dn, preferred_element_type=F32)
              + lax.dot_general(wh, xl, dn, preferred_element_type=F32)
              + lax.dot_general(wl_ref[...], xh, dn, preferred_element_type=F32))
    scores = jax.nn.sigmoid(logits)
    choice = scores + bias_ref[...]

    per = n_exp // N_EXPERT_GROUPS
    neg = jnp.float32(-jnp.inf)
    sub = lax.broadcasted_iota(I32, (per, tm), 0)
    gscore = []
    for g in range(N_EXPERT_GROUPS):
        cg = choice[g * per:(g + 1) * per, :]
        m1 = jnp.max(cg, axis=0, keepdims=True)
        i1 = jnp.min(jnp.where(cg == m1, sub, per), axis=0, keepdims=True)
        m2 = jnp.max(jnp.where(sub == i1, neg, cg), axis=0, keepdims=True)
        gscore.append(m1 + m2)
    masked_parts = []
    for g in range(N_EXPERT_GROUPS):
        beat = jnp.zeros((1, tm), I32)
        for h in range(N_EXPERT_GROUPS):
            if h == g:
                continue
            wins = (gscore[h] > gscore[g]) | ((gscore[h] == gscore[g]) & (h < g))
            beat = beat + wins.astype(I32)
        keep = beat < TOPK_GROUPS
        cg = choice[g * per:(g + 1) * per, :]
        masked_parts.append(jnp.where(keep, cg, neg))
    masked = jnp.concatenate(masked_parts, axis=0)

    eidx = lax.broadcasted_iota(I32, (n_exp, tm), 0)
    hot = jnp.zeros((n_exp, tm), F32)
    sel_idx, sel_w, sel_hot = [], [], []
    for _ in range(TOP_K):
        mk = jnp.max(masked, axis=0, keepdims=True)
        ik = jnp.min(jnp.where(masked == mk, eidx, n_exp), axis=0, keepdims=True)
        one = eidx == ik
        sel_idx.append(ik)
        sel_w.append(jnp.sum(jnp.where(one, scores, 0.0), axis=0, keepdims=True))
        sel_hot.append(one)
        hot = hot + one.astype(F32)
        masked = jnp.where(one, neg, masked)

    wsum = sel_w[0]
    for k in range(1, TOP_K):
        wsum = wsum + sel_w[k]

    r = lax.broadcasted_iota(I32, (tm, tm), 0)
    c = lax.broadcasted_iota(I32, (tm, tm), 1)
    upper = (r < c).astype(BF16)
    pref = jnp.dot(hot.astype(BF16), upper, preferred_element_type=F32) + carry_ref[...]
    for k in range(TOP_K):
        idx_ref[k:k + 1, :] = sel_idx[k]
        wt_ref[k:k + 1, :] = sel_w[k] / wsum * ROUTED_SCALE
        rk = jnp.sum(jnp.where(sel_hot[k], pref, 0.0), axis=0, keepdims=True)
        rank_ref[k:k + 1, :] = rk.astype(I32)
    carry_ref[...] = carry_ref[...] + jnp.sum(hot, axis=1, keepdims=True)
    cnt_ref[...] = carry_ref[...]


def _router(x1, wh, wl, bias_col):
    T, D = x1.shape
    n_exp = wh.shape[0]
    tm = _pick(T, (512, 256, 128))
    kern = functools.partial(_router_kernel, tm=tm, n_exp=n_exp)
    return pl.pallas_call(
        kern,
        grid=(T // tm,),
        in_specs=[
            pl.BlockSpec((tm, D), lambda i: (i, 0)),
            pl.BlockSpec((n_exp, D), lambda i: (0, 0)),
            pl.BlockSpec((n_exp, D), lambda i: (0, 0)),
            pl.BlockSpec((n_exp, 1), lambda i: (0, 0)),
        ],
        out_specs=[
            pl.BlockSpec((TOP_K, tm), lambda i: (0, i)),
            pl.BlockSpec((TOP_K, tm), lambda i: (0, i)),
            pl.BlockSpec((TOP_K, tm), lambda i: (0, i)),
            pl.BlockSpec((n_exp, 1), lambda i: (0, 0)),
        ],
        out_shape=[
            jax.ShapeDtypeStruct((TOP_K, T), I32),
            jax.ShapeDtypeStruct((TOP_K, T), F32),
            jax.ShapeDtypeStruct((TOP_K, T), I32),
            jax.ShapeDtypeStruct((n_exp, 1), F32),
        ],
        scratch_shapes=[pltpu.VMEM((n_exp, 1), F32)],
        compiler_params=_cparams(("arbitrary",)),
        name="router_topk",
    )(x1, wh, wl, bias_col)


def _dest_kernel(idx_ref, rank_ref, off_ref, o_ref, *, tm, n_exp):
    eidx = lax.broadcasted_iota(I32, (n_exp, tm), 0)
    off = off_ref[...]
    for k in range(TOP_K):
        hit = eidx == idx_ref[k:k + 1, :]
        base = jnp.sum(jnp.where(hit, off, 0.0), axis=0, keepdims=True)
        o_ref[k:k + 1, :] = base.astype(I32) + rank_ref[k:k + 1, :]


def _dest_rows(eidx, rank, pad_off_col):
    K, T = eidx.shape
    n_exp = pad_off_col.shape[0]
    tm = _pick(T, (1024, 512, 256, 128))
    kern = functools.partial(_dest_kernel, tm=tm, n_exp=n_exp)
    return pl.pallas_call(
        kern,
        grid=(T // tm,),
        in_specs=[
            pl.BlockSpec((K, tm), lambda i: (0, i)),
            pl.BlockSpec((K, tm), lambda i: (0, i)),
            pl.BlockSpec((n_exp, 1), lambda i: (0, 0)),
        ],
        out_specs=pl.BlockSpec((K, tm), lambda i: (0, i)),
        out_shape=jax.ShapeDtypeStruct((K, T), I32),
        compiler_params=_cparams(("arbitrary",)),
        name="moe_dest_rows",
    )(eidx, rank, pad_off_col)


def _dispatch_kernel(cnt_ref, off_ref, nbu_ref, dest_ref, x_ref, xs_ref, zero_ref, sem, *,
                     tm, blk, n_exp, exp_per_step, n_blocks, tail_per_step):
    step = pl.program_id(0)

    def tile(ref, r):
        return ref.at[pl.ds(pl.multiple_of(r * ROW_SUB, ROW_SUB), ROW_SUB), :]

    def issue(t, carry):
        src = tile(x_ref, t)
        for k in range(TOP_K):
            pltpu.make_async_copy(src, tile(xs_ref, dest_ref[t * TOP_K + k]), sem).start(priority=k % 2)
        return carry

    lax.fori_loop(0, tm, issue, 0)

    zero_ref[...] = jnp.zeros(zero_ref.shape, U32)

    def zero_copy(d):
        return pltpu.make_async_copy(tile(zero_ref, 0), tile(xs_ref, d), sem)

    def zero_block(bi):
        rows = pl.ds(pl.multiple_of(bi * (blk * ROW_SUB), blk * ROW_SUB), blk * ROW_SUB)
        return pltpu.make_async_copy(zero_ref, xs_ref.at[rows, :], sem)

    tail_blocks = []
    for j in range(tail_per_step):
        bi = nbu_ref[0] + step * tail_per_step + j
        tail_blocks.append(bi)

        @pl.when(bi < n_blocks)
        def _():
            zero_block(bi).start()

    n_pad_total = jnp.int32(0)
    for j in range(exp_per_step):
        e = jnp.minimum(step * exp_per_step + j, n_exp - 1)
        valid = (step * exp_per_step + j) < n_exp
        cnt = cnt_ref[e]
        n_pad = jnp.where(valid, (blk - cnt % blk) % blk, 0)
        base = off_ref[e] + cnt

        def zissue(i, carry, base=base):
            zero_copy(base + i).start()
            return carry

        lax.fori_loop(0, n_pad, zissue, 0)
        n_pad_total = n_pad_total + n_pad

    for k in range(TOP_K):
        pltpu.make_async_copy(x_ref, xs_ref.at[pl.ds(0, tm * ROW_SUB), :], sem).wait()

    def wait(i, carry):
        zero_copy(0).wait()
        return carry

    lax.fori_loop(0, n_pad_total, wait, 0)

    for bi in tail_blocks:
        @pl.when(bi < n_blocks)
        def _():
            zero_block(0).wait()


def _dispatch(counts, pad_off, nb_used, dest_flat, x1p, n_rows_out, blk):
    T = x1p.shape[0] // ROW_SUB
    n_exp = counts.shape[0]
    tm = _pick(T, (256, 128))
    n_steps = T // tm
    exp_per_step = -(-n_exp // n_steps)
    n_blocks = n_rows_out // blk
    max_tail = n_blocks - (T * TOP_K) // blk
    tail_per_step = -(-max_tail // n_steps)
    kern = functools.partial(_dispatch_kernel, tm=tm, blk=blk, n_exp=n_exp,
                             exp_per_step=exp_per_step, n_blocks=n_blocks,
                             tail_per_step=tail_per_step)
    grid_spec = pltpu.PrefetchScalarGridSpec(
        num_scalar_prefetch=3,
        grid=(n_steps,),
        in_specs=[
            pl.BlockSpec((tm * TOP_K,), lambda i, c, o, n: (i,), memory_space=pltpu.SMEM),
            pl.BlockSpec((tm * ROW_SUB, LANES), lambda i, c, o, n: (i, 0)),
        ],
        out_specs=pl.BlockSpec(memory_space=pl.ANY),
        scratch_shapes=[pltpu.VMEM((blk * ROW_SUB, LANES), U32), pltpu.SemaphoreType.DMA(())],
    )
    return pl.pallas_call(
        kern,
        grid_spec=grid_spec,
        out_shape=jax.ShapeDtypeStruct((n_rows_out * ROW_SUB, LANES), U32),
        compiler_params=_cparams(("arbitrary",)),
        name="moe_dispatch",
    )(counts, pad_off, nb_used, dest_flat, x1p)


def _gmm_kernel(be_ref, nb_ref, fresh_ref, slot_ref, next_ref,
                xs_ref, wg_ref, wu_ref, wd_ref, ys_ref,
                wg_f, wu_f, wd_f, wg_b, wu_b, wd_b, sem):
    b = pl.program_id(0)
    live = b < nb_ref[0]

    def fetch(e, s):
        return (pltpu.make_async_copy(wg_ref.at[e], wg_f.at[s], sem.at[s]),
                pltpu.make_async_copy(wu_ref.at[e], wu_f.at[s], sem.at[s]),
                pltpu.make_async_copy(wd_ref.at[e], wd_f.at[s], sem.at[s]))

    @pl.when(b == 0)
    def _():
        for c in fetch(be_ref[0], 0):
            c.start()

    @pl.when(live & (fresh_ref[b] == 1))
    def _():
        s = slot_ref[b]
        for c in fetch(be_ref[b], s):
            c.wait()
        nxt = next_ref[b]

        @pl.when(nxt >= 0)
        def _():
            for c in fetch(nxt, 1 - s):
                c.start()

        wg_b[...] = wg_f[s].astype(BF16)
        wu_b[...] = wu_f[s].astype(BF16)
        wd_b[...] = wd_f[s].astype(BF16)

    @pl.when(live)
    def _():
        blk = xs_ref.shape[0] // ROW_SUB
        lo, hi = _unpack_pairs(_load_tiles_as_rows(xs_ref, blk))
        lo = lo.astype(BF16)
        hi = hi.astype(BF16)
        half = lo.shape[1]
        g = (jnp.dot(lo, wg_b[:half, :], preferred_element_type=F32)
             + jnp.dot(hi, wg_b[half:, :], preferred_element_type=F32))
        u = (jnp.dot(lo, wu_b[:half, :], preferred_element_type=F32)
             + jnp.dot(hi, wu_b[half:, :], preferred_element_type=F32))
        h = (g * jax.nn.sigmoid(g) * u).astype(BF16)
        y_lo = jnp.dot(h, wd_b[:, :half], preferred_element_type=F32)
        y_hi = jnp.dot(h, wd_b[:, half:], preferred_element_type=F32)
        _store_rows_as_tiles(ys_ref, _pack_halves(y_lo, y_hi))

    @pl.when(jnp.logical_not(live))
    def _():
        ys_ref[...] = jnp.zeros(ys_ref.shape, U32)


def _gmm(block_expert, nb_used, fresh, slot, next_expert, xs, wg, wu, wd, blk):
    P = xs.shape[0] // ROW_SUB
    n_exp, D, FF = wg.shape
    nb = P // blk

    def row_map(b, be, nbu, fr, sl, nx):
        return (jnp.minimum(b, nbu[0] - 1), 0)

    grid_spec = pltpu.PrefetchScalarGridSpec(
        num_scalar_prefetch=5,
        grid=(nb,),
        in_specs=[
            pl.BlockSpec((blk * ROW_SUB, LANES), row_map),
            pl.BlockSpec(memory_space=pl.ANY),
            pl.BlockSpec(memory_space=pl.ANY),
            pl.BlockSpec(memory_space=pl.ANY),
        ],
        out_specs=pl.BlockSpec((blk * ROW_SUB, LANES), lambda b, be, nbu, fr, sl, nx: (b, 0)),
        scratch_shapes=[
            pltpu.VMEM((2, D, FF), F32), pltpu.VMEM((2, D, FF), F32), pltpu.VMEM((2, FF, D), F32),
            pltpu.VMEM((D, FF), BF16), pltpu.VMEM((D, FF), BF16), pltpu.VMEM((FF, D), BF16),
            pltpu.SemaphoreType.DMA((2,)),
        ],
    )
    return pl.pallas_call(
        _gmm_kernel,
        grid_spec=grid_spec,
        out_shape=jax.ShapeDtypeStruct((P * ROW_SUB, LANES), U32),
        compiler_params=_cparams(("arbitrary",)),
        name="moe_grouped_swiglu",
    )(block_expert, nb_used, fresh, slot, next_expert, xs, wg, wu, wd)


def _combine_kernel(dest_ref, wt_ref, x_ref, ys_ref, wg_ref, wu_ref, wd_ref, g_ref, b_ref,
                    o_ref, buf, sem, *, tm):
    def tile(ref, r):
        return ref.at[pl.ds(pl.multiple_of(r * ROW_SUB, ROW_SUB), ROW_SUB), :]

    def issue(t, carry):
        for k in range(TOP_K):
            pltpu.make_async_copy(tile(ys_ref, dest_ref[t * TOP_K + k]), tile(buf.at[k], t),
                                  sem).start(priority=k % 2)
        return carry

    lax.fori_loop(0, tm, issue, 0)

    x = x_ref[...]
    xb = x.astype(BF16)
    g = jnp.dot(xb, wg_ref[...], preferred_element_type=F32)
    u = jnp.dot(xb, wu_ref[...], preferred_element_type=F32)
    h = (g * jax.nn.sigmoid(g) * u).astype(BF16)
    shared = jnp.dot(h, wd_ref[...], preferred_element_type=F32)

    for k in range(TOP_K):
        pltpu.make_async_copy(ys_ref.at[pl.ds(0, tm * ROW_SUB), :], buf.at[k], sem).wait()

    half = x.shape[1] // 2
    wt = wt_ref[...]
    r_lo = jnp.zeros((tm, half), F32)
    r_hi = jnp.zeros((tm, half), F32)
    for k in range(TOP_K):
        lo, hi = _unpack_pairs(_load_tiles_as_rows(buf.at[k], tm))
        wk = wt[:, k:k + 1]
        r_lo = r_lo + lo * wk
        r_hi = r_hi + hi * wk
    routed = jnp.concatenate([r_lo, r_hi], axis=1)
    y = ALPHA * x + (routed + shared)
    o_ref[...] = _layer_norm(y, g_ref[...], b_ref[...])


def _combine(dest_flat, wt_tok, x1, ys, wg_s, wu_s, wd_s, g, b):
    T, D = x1.shape
    FF = wg_s.shape[1]
    tm = _pick(T, (256, 128))
    n_steps = T // tm
    kern = functools.partial(_combine_kernel, tm=tm)
    return pl.pallas_call(
        kern,
        grid=(n_steps,),
        in_specs=[
            pl.BlockSpec((tm * TOP_K,), lambda i: (i,), memory_space=pltpu.SMEM),
            pl.BlockSpec((tm, TOP_K), lambda i: (i, 0)),
            pl.BlockSpec((tm, D), lambda i: (i, 0)),
            pl.BlockSpec(memory_space=pl.ANY),
            pl.BlockSpec((D, FF), lambda i: (0, 0)),
            pl.BlockSpec((D, FF), lambda i: (0, 0)),
            pl.BlockSpec((FF, D), lambda i: (0, 0)),
            pl.BlockSpec((1, D), lambda i: (0, 0)),
            pl.BlockSpec((1, D), lambda i: (0, 0)),
        ],
        out_specs=pl.BlockSpec((tm, D), lambda i: (i, 0)),
        out_shape=jax.ShapeDtypeStruct((T, D), F32),
        scratch_shapes=[pltpu.VMEM((TOP_K, tm * ROW_SUB, LANES), U32), pltpu.SemaphoreType.DMA(())],
        compiler_params=_cparams(("arbitrary",)),
        name="moe_combine_ln2",
    )(dest_flat, wt_tok, x1, ys, wg_s, wu_s, wd_s, g, b)


def _rope_tables(positions):
    half = ROT_DIM // 2
    inv_freq = ROPE_THETA ** (-jnp.arange(0, ROT_DIM, 2, dtype=F32) / ROT_DIM)
    ang = positions.reshape(-1).astype(F32)[:, None] * inv_freq
    cos = jnp.cos(ang)
    sin = jnp.sin(ang)
    T = ang.shape[0]
    pad = HEAD_DIM - ROT_DIM
    c64 = jnp.concatenate([cos, cos, jnp.ones((T, pad), F32)], axis=1)
    sa64 = jnp.concatenate([jnp.zeros((T, half), F32), sin, jnp.zeros((T, pad), F32)], axis=1)
    sb64 = jnp.concatenate([-sin, jnp.zeros((T, half + pad), F32)], axis=1)
    rep = LANES // HEAD_DIM
    return jnp.tile(c64, (1, rep)), jnp.tile(sa64, (1, rep)), jnp.tile(sb64, (1, rep))


def kernel(x, positions, w_in, lam_q1, lam_k1, lam_q2, lam_k2, subln_g, sgu_ln_g, sgu_ln_b,
           w_spatial, b_spatial, w_out, ln1_g, ln1_b, w_router, router_bias, w_gate_exp,
           w_up_exp, w_down_exp, w_gate_sh, w_up_sh, w_down_sh, ln2_g, ln2_b):
    B, S, D = x.shape
    T = B * S
    l = 0
    x2 = x.reshape(T, D)
    n_exp = w_router.shape[-1]
    blk = 256

    w_in_b = w_in[l].astype(BF16)
    w_out_b = w_out[l].astype(BF16)
    rc, rsa, rsb = _rope_tables(positions)
    lam = (jnp.exp(jnp.sum(lam_q1[l].astype(F32) * lam_k1[l].astype(F32)))
           - jnp.exp(jnp.sum(lam_q2[l].astype(F32) * lam_k2[l].astype(F32)))
           + LAMBDA_INIT).reshape(1, 1).astype(F32)
    bs_full = jnp.broadcast_to(b_spatial[l][:, :, None], (N_SG_GROUPS, SG_CHUNK, SG_DIM)).astype(F32)
    w_rt = w_router[l].astype(F32).T
    w_rt_hi = w_rt.astype(BF16)
    w_rt_lo = (w_rt - w_rt_hi.astype(F32)).astype(BF16)
    bias_col = router_bias[l].astype(F32).reshape(n_exp, 1)

    proj = _inproj(x2, w_in_b, rc, rsa, rsb)
    attn = _attention(proj, lam, subln_g[l].reshape(1, V_DIM).astype(F32), B, S)
    sg = _sgu(proj, sgu_ln_g[l].astype(F32), sgu_ln_b[l].astype(F32), w_spatial[l], bs_full)
    x1, x1p = _outproj(attn, sg, x2, w_out_b, ln1_g[l].reshape(1, D), ln1_b[l].reshape(1, D))

    eidx, wt, rank, cnt = _router(x1, w_rt_hi, w_rt_lo, bias_col)

    counts = cnt.reshape(n_exp).astype(I32)
    padded = ((counts + blk - 1) // blk) * blk
    pad_end = jnp.cumsum(padded)
    pad_off = pad_end - padded
    P = T * TOP_K + n_exp * blk
    assert P < 2 ** 24
    dest = _dest_rows(eidx, rank, pad_off.astype(F32).reshape(n_exp, 1))
    dest_flat = dest.T.reshape(T * TOP_K)
    wt_tok = wt.T
    nb = P // blk
    block_start = jnp.arange(nb, dtype=I32) * blk
    block_expert = jnp.minimum(
        jnp.sum((pad_end[None, :] <= block_start[:, None]).astype(I32), axis=1), n_exp - 1)
    nb_used = (pad_end[-1] // blk).reshape(1).astype(I32)
    live_blk = jnp.arange(nb, dtype=I32) < nb_used[0]
    fresh = jnp.concatenate([jnp.ones((1,), I32),
                             (block_expert[1:] != block_expert[:-1]).astype(I32)])
    slot = ((jnp.cumsum(fresh) - 1) % 2).astype(I32)
    later = live_blk[None, :] & (block_expert[None, :] > block_expert[:, None])
    next_expert = jnp.min(jnp.where(later, block_expert[None, :], n_exp), axis=1)
    next_expert = jnp.where(next_expert < n_exp, next_expert, -1).astype(I32)

    xs = _dispatch(counts, pad_off.astype(I32), nb_used, dest_flat, x1p, P, blk)
    ys = _gmm(block_expert, nb_used, fresh, slot, next_expert, xs,
              w_gate_exp[l], w_up_exp[l], w_down_exp[l], blk)
    out = _combine(dest_flat, wt_tok, x1, ys,
                   w_gate_sh[l].astype(BF16), w_up_sh[l].astype(BF16), w_down_sh[l].astype(BF16),
                   ln2_g[l].reshape(1, D), ln2_b[l].reshape(1, D))
    return out.reshape(B, S, D)
```

```python
import functools
import math

import jax
import jax.numpy as jnp
from jax import lax
from jax.experimental import pallas as pl
from jax.experimental.pallas import tpu as pltpu

F32 = jnp.float32
BF16 = jnp.bfloat16
U32 = jnp.uint32
I32 = jnp.int32

N_HEADS = 8
HEAD_DIM = 64
V_DIM = 2 * HEAD_DIM
N_SG_GROUPS = 8
SG_DIM = 128
SG_CHUNK = 128
ROPE_THETA = 500000.0
ROT_DIM = HEAD_DIM // 4
TOP_K = 8
N_EXPERT_GROUPS = 8
TOPK_GROUPS = 4
ROUTED_SCALE = 2.5
LN_EPS = 1e-5
DEPTH = 1
ALPHA = (2 * DEPTH) ** 0.25
LAMBDA_INIT = 0.8 - 0.6 * math.exp(-0.3 * 0)
QK_SCALE = HEAD_DIM ** -0.5 * math.log2(math.e)

LANES = 128
SEG = 1024
ATTN_TQ = (512, 256, 128)
ATTN_TK = (1024, 512, 256, 128)
MOE_BLOCK_ROWS = 256
VMEM_LIMIT = 56 * 1024 * 1024


def _cparams(sem, vmem=VMEM_LIMIT):
    return pltpu.CompilerParams(dimension_semantics=sem, vmem_limit_bytes=vmem)


def _pick(n, prefs):
    for p in prefs:
        if n % p == 0:
            return p
    return n


def _inproj_kernel(x_ref, w_ref, c_ref, sa_ref, sb_ref, o_ref, xb_ref):
    j = pl.program_id(1)

    @pl.when(j == 0)
    def _():
        xb_ref[...] = x_ref[...].astype(BF16)

    acc = jnp.dot(xb_ref[...], w_ref[...], preferred_element_type=F32)

    @pl.when(j < 2)
    def _():
        scale = jnp.where(j == 0, QK_SCALE, 1.0).astype(F32)
        c = c_ref[...] * scale
        sa = sa_ref[...] * scale
        sb = sb_ref[...] * scale
        for blk in range(SEG // LANES):
            a = acc[:, blk * LANES:(blk + 1) * LANES]
            r = (a * c + pltpu.roll(a, ROT_DIM // 2, 1) * sa
                 + pltpu.roll(a, LANES - ROT_DIM // 2, 1) * sb)
            o_ref[:, blk * LANES:(blk + 1) * LANES] = r.astype(BF16)

    @pl.when(j == 2)
    def _():
        o_ref[...] = acc.astype(BF16)

    @pl.when(j > 2)
    def _():
        g = 0.5 * acc * (1.0 + lax.erf(acc * (2.0 ** -0.5)))
        o_ref[...] = g.astype(BF16)


def _inproj(x2, w_b, rc, rsa, rsb):
    T, D = x2.shape
    n_seg = w_b.shape[1] // SEG
    tm = _pick(T, (1024, 512, 256, 128))
    return pl.pallas_call(
        _inproj_kernel,
        grid=(T // tm, n_seg),
        in_specs=[
            pl.BlockSpec((tm, D), lambda i, j: (i, 0)),
            pl.BlockSpec((D, SEG), lambda i, j: (0, j)),
            pl.BlockSpec((tm, LANES), lambda i, j: (i, 0)),
            pl.BlockSpec((tm, LANES), lambda i, j: (i, 0)),
            pl.BlockSpec((tm, LANES), lambda i, j: (i, 0)),
        ],
        out_specs=pl.BlockSpec((tm, SEG), lambda i, j: (i, j)),
        out_shape=jax.ShapeDtypeStruct((T, n_seg * SEG), BF16),
        scratch_shapes=[pltpu.VMEM((tm, D), BF16)],
        compiler_params=_cparams(("arbitrary", "arbitrary")),
        name="inproj",
    )(x2, w_b, rc, rsa, rsb)


def _attn_kernel(lam_ref, q_ref, k_ref, v_ref, g_ref, o_ref,
                 acc_a, acc_b, m_a, m_b, s_a0, s_a1, s_b0, s_b1, *, tq, tk):
    qi = pl.program_id(2)
    s_a = (s_a0, s_a1)
    s_b = (s_b0, s_b1)
    dn = (((1,), (1,)), ((), ()))
    q = q_ref[...]
    lane = lax.broadcasted_iota(I32, q.shape, 1)
    zero = jnp.zeros_like(q)
    q_a = jnp.where(lane < HEAD_DIM, q, zero)
    q_b = jnp.where(lane >= HEAD_DIM, q, zero)

    for r in (acc_a, acc_b):
        r[...] = jnp.zeros(r.shape, F32)
    for r in (m_a, m_b):
        r[...] = jnp.full(r.shape, -jnp.inf, F32)

    def scores(ki, slot):
        start = pl.multiple_of(ki * tk, tk)
        k = k_ref[pl.ds(start, tk), :]
        s_a[slot][...] = lax.dot_general(q_a, k, dn, preferred_element_type=F32)
        s_b[slot][...] = lax.dot_general(q_b, k, dn, preferred_element_type=F32)


    def consume(ki, slot, masked):
        start = pl.multiple_of(ki * tk, tk)
        v = v_ref[pl.ds(start, tk), :]
        v_ext = jnp.concatenate([v, jnp.ones_like(v)], axis=1)
        if masked:
            row = lax.broadcasted_iota(I32, (tq, tk), 0) + (qi * tq - start)
            col = lax.broadcasted_iota(I32, (tq, tk), 1)
            keep = col <= row
        for s_ref, acc, m in ((s_a[slot], acc_a, m_a), (s_b[slot], acc_b, m_b)):
            s = s_ref[...]
            if masked:
                s = jnp.where(keep, s, -jnp.inf)
            m_prev = m[...]
            m_new = jnp.maximum(m_prev, jnp.max(s, axis=1, keepdims=True))
            alpha = jnp.exp2(m_prev - m_new)
            p = jnp.exp2(s - m_new)
            acc[...] = alpha * acc[...] + jnp.dot(p.astype(BF16), v_ext, preferred_element_type=F32)
            m[...] = m_new

    n_full = (qi * tq) // tk
    n_pairs = n_full // 2
    scores(0, 0)

    def body(t, carry):
        ki = 2 * t
        scores(ki + 1, 1)
        consume(ki, 0, False)
        scores(ki + 2, 0)
        consume(ki + 1, 1, False)
        return carry

    lax.fori_loop(0, n_pairs, body, 0)

    @pl.when(n_full % 2 == 1)
    def _():
        scores(n_full, 1)
        consume(n_full - 1, 0, False)
        consume(n_full, 1, True)

    @pl.when(n_full % 2 == 0)
    def _():
        consume(n_full, 0, True)

    lam = lam_ref[0, 0]
    o = (acc_a[:, :V_DIM] / acc_a[:, V_DIM:]) - lam * (acc_b[:, :V_DIM] / acc_b[:, V_DIM:])
    ms = jnp.mean(o * o, axis=1, keepdims=True)
    o = o * lax.rsqrt(ms + LN_EPS) * g_ref[...] * (1.0 - LAMBDA_INIT)
    o_ref[...] = o.astype(BF16)


def _attention(proj, lam, subln_g, B, S):
    T = B * S
    tq = _pick(S, ATTN_TQ)
    nq = S // tq
    tk = _pick(S, ATTN_TK)
    assert tk % tq == 0
    kern = functools.partial(_attn_kernel, tq=tq, tk=tk)
    return pl.pallas_call(
        kern,
        grid=(B, N_HEADS, nq),
        in_specs=[
            pl.BlockSpec(memory_space=pltpu.SMEM),
            pl.BlockSpec((tq, V_DIM), lambda b, h, i: (b * nq + i, h)),
            pl.BlockSpec((S, V_DIM), lambda b, h, i: (b, N_HEADS + h)),
            pl.BlockSpec((S, V_DIM), lambda b, h, i: (b, 2 * N_HEADS + h)),
            pl.BlockSpec((1, V_DIM), lambda b, h, i: (0, 0)),
        ],
        out_specs=pl.BlockSpec((tq, V_DIM), lambda b, h, i: (b * nq + i, h)),
        out_shape=jax.ShapeDtypeStruct((T, N_HEADS * V_DIM), BF16),
        scratch_shapes=[
            pltpu.VMEM((tq, 2 * V_DIM), F32), pltpu.VMEM((tq, 2 * V_DIM), F32),
            pltpu.VMEM((tq, 1), F32), pltpu.VMEM((tq, 1), F32),
            pltpu.VMEM((tq, tk), F32), pltpu.VMEM((tq, tk), F32),
            pltpu.VMEM((tq, tk), F32), pltpu.VMEM((tq, tk), F32),
        ],
        compiler_params=_cparams(("arbitrary", "arbitrary", "arbitrary")),
        name="diff_attention",
    )(lam, proj, proj, proj, subln_g)


def _sgu_kernel(u_ref, v_ref, g_ref, b_ref, w_ref, bs_ref, o_ref, *, tm):
    g = pl.program_id(1)
    ln_g = g_ref[pl.ds(g, 1), :]
    ln_b = b_ref[pl.ds(g, 1), :]
    row = lax.broadcasted_iota(I32, (SG_CHUNK, SG_CHUNK), 0)
    col = lax.broadcasted_iota(I32, (SG_CHUNK, SG_CHUNK), 1)
    w = jnp.where(col <= row, w_ref[0], 0.0).astype(BF16)
    bs = bs_ref[0]
    for c in range(tm // SG_CHUNK):
        sl = slice(c * SG_CHUNK, (c + 1) * SG_CHUNK)
        v = v_ref[sl, :].astype(F32)
        mu = jnp.mean(v, axis=1, keepdims=True)
        vc = v - mu
        var = jnp.mean(vc * vc, axis=1, keepdims=True)
        vn = vc * lax.rsqrt(var + LN_EPS) * ln_g + ln_b
        mixed = jnp.dot(w, vn.astype(BF16), preferred_element_type=F32) + bs
        o_ref[sl, :] = (u_ref[sl, :].astype(F32) * mixed).astype(BF16)


def _sgu(proj, ln_g, ln_b, w_s, bs_full):
    T = proj.shape[0]
    tm = _pick(T, (2048, 1024, 512, 256, 128))
    u_blk = 3 * SEG // SG_DIM
    v_blk = 4 * SEG // SG_DIM
    kern = functools.partial(_sgu_kernel, tm=tm)
    return pl.pallas_call(
        kern,
        grid=(T // tm, N_SG_GROUPS),
        in_specs=[
            pl.BlockSpec((tm, SG_DIM), lambda i, g: (i, u_blk + g)),
            pl.BlockSpec((tm, SG_DIM), lambda i, g: (i, v_blk + g)),
            pl.BlockSpec((N_SG_GROUPS, SG_DIM), lambda i, g: (0, 0)),
            pl.BlockSpec((N_SG_GROUPS, SG_DIM), lambda i, g: (0, 0)),
            pl.BlockSpec((1, SG_CHUNK, SG_CHUNK), lambda i, g: (g, 0, 0)),
            pl.BlockSpec((1, SG_CHUNK, SG_DIM), lambda i, g: (g, 0, 0)),
        ],
        out_specs=pl.BlockSpec((tm, SG_DIM), lambda i, g: (i, g)),
        out_shape=jax.ShapeDtypeStruct((T, N_SG_GROUPS * SG_DIM), BF16),
        compiler_params=_cparams(("arbitrary", "arbitrary")),
        name="spatial_gating",
    )(proj, proj, ln_g, ln_b, w_s, bs_full)


def _pack_halves(y_lo, y_hi):
    lo = lax.bitcast_convert_type(y_lo.astype(BF16).astype(F32), U32)
    hi = lax.bitcast_convert_type(y_hi.astype(BF16).astype(F32), U32)
    return (lo >> 16) | (hi & jnp.uint32(0xFFFF0000))


def _pack_pairs(y):
    half = y.shape[1] // 2
    return _pack_halves(y[:, :half], y[:, half:])


def _unpack_pairs(p):
    lo = lax.bitcast_convert_type(p << 16, F32)
    hi = lax.bitcast_convert_type(p & jnp.uint32(0xFFFF0000), F32)
    return lo, hi


ROW_SUB = 8


def _store_rows_as_tiles(ref, packed):
    n = packed.shape[0]
    for s in range(ROW_SUB):
        ref[pl.ds(s, n, stride=ROW_SUB), :] = packed[:, s * LANES:(s + 1) * LANES]


def _load_tiles_as_rows(ref, n):
    return jnp.concatenate(
        [ref[pl.ds(s, n, stride=ROW_SUB), :] for s in range(ROW_SUB)], axis=1)


def _layer_norm(y, g, b):
    mu = jnp.mean(y, axis=1, keepdims=True)
    yc = y - mu
    var = jnp.mean(yc * yc, axis=1, keepdims=True)
    return yc * lax.rsqrt(var + LN_EPS) * g + b


def _outproj_router_kernel(a_ref, s_ref, x_ref, w_ref, g_ref, b_ref, wh_ref, wl_ref, bias_ref,
                           x1_ref, xp_ref, idx_ref, wt_ref, rank_ref, cnt_ref, carry_ref, *,
                           tm, n_exp):
    half = w_ref.shape[0] // 2
    mix = jnp.dot(a_ref[...], w_ref[:half, :], preferred_element_type=F32)
    mix = mix + jnp.dot(s_ref[...], w_ref[half:, :], preferred_element_type=F32)
    y = ALPHA * x_ref[...] + mix
    x1 = _layer_norm(y, g_ref[...], b_ref[...])
    x1_ref[...] = x1
    _store_rows_as_tiles(xp_ref, _pack_pairs(x1))
    _route_tile(x1, wh_ref, wl_ref, bias_ref, idx_ref, wt_ref, rank_ref, cnt_ref, carry_ref,
                tm=tm, n_exp=n_exp)


def _outproj_router(attn, sg, x2, w_b, g, b, wh, wl, bias_col):
    T, D = x2.shape
    assert D // 2 == ROW_SUB * LANES
    n_exp = wh.shape[0]
    tm = _pick(T, (512, 256, 128))
    wa = attn.shape[1]
    ws = sg.shape[1]
    kern = functools.partial(_outproj_router_kernel, tm=tm, n_exp=n_exp)
    return pl.pallas_call(
        kern,
        grid=(T // tm,),
        in_specs=[
            pl.BlockSpec((tm, wa), lambda i: (i, 0)),
            pl.BlockSpec((tm, ws), lambda i: (i, 0)),
            pl.BlockSpec((tm, D), lambda i: (i, 0)),
            pl.BlockSpec((wa + ws, D), lambda i: (0, 0)),
            pl.BlockSpec((1, D), lambda i: (0, 0)),
            pl.BlockSpec((1, D), lambda i: (0, 0)),
            pl.BlockSpec((n_exp, D), lambda i: (0, 0)),
            pl.BlockSpec((n_exp, D), lambda i: (0, 0)),
            pl.BlockSpec((n_exp, 1), lambda i: (0, 0)),
        ],
        out_specs=[
            pl.BlockSpec((tm, D), lambda i: (i, 0)),
            pl.BlockSpec((tm * ROW_SUB, LANES), lambda i: (i, 0)),
            pl.BlockSpec((TOP_K, tm), lambda i: (0, i)),
            pl.BlockSpec((TOP_K, tm), lambda i: (0, i)),
            pl.BlockSpec((TOP_K, tm), lambda i: (0, i)),
            pl.BlockSpec((n_exp, 1), lambda i: (0, 0)),
        ],
        out_shape=[
            jax.ShapeDtypeStruct((T, D), F32),
            jax.ShapeDtypeStruct((T * ROW_SUB, LANES), U32),
            jax.ShapeDtypeStruct((TOP_K, T), I32),
            jax.ShapeDtypeStruct((TOP_K, T), F32),
            jax.ShapeDtypeStruct((TOP_K, T), I32),
            jax.ShapeDtypeStruct((n_exp, 1), F32),
        ],
        scratch_shapes=[pltpu.VMEM((n_exp, 1), F32)],
        compiler_params=_cparams(("arbitrary",)),
        name="outproj_ln1_router",
    )(attn, sg, x2, w_b, g, b, wh, wl, bias_col)


def _route_tile(x, wh_ref, wl_ref, bias_ref, idx_ref, wt_ref, rank_ref, cnt_ref,
                carry_ref, *, tm, n_exp):
    step = pl.program_id(0)

    @pl.when(step == 0)
    def _():
        carry_ref[...] = jnp.zeros(carry_ref.shape, F32)

    xh = x.astype(BF16)
    xl = (x - xh.astype(F32)).astype(BF16)
    dn = (((1,), (1,)), ((), ()))
    wh = wh_ref[...]
    logits = (lax.dot_general(wh, xh, dn, preferred_element_type=F32)
              + lax.dot_general(wh, xl, dn, preferred_element_type=F32)
              + lax.dot_general(wl_ref[...], xh, dn, preferred_element_type=F32))
    scores = jax.nn.sigmoid(logits)
    choice = scores + bias_ref[...]

    per = n_exp // N_EXPERT_GROUPS
    neg = jnp.float32(-jnp.inf)
    sub = lax.broadcasted_iota(I32, (per, tm), 0)
    gscore = []
    for g in range(N_EXPERT_GROUPS):
        cg = choice[g * per:(g + 1) * per, :]
        m1 = jnp.max(cg, axis=0, keepdims=True)
        i1 = jnp.min(jnp.where(cg == m1, sub, per), axis=0, keepdims=True)
        m2 = jnp.max(jnp.where(sub == i1, neg, cg), axis=0, keepdims=True)
        gscore.append(m1 + m2)
    masked_parts = []
    for g in range(N_EXPERT_GROUPS):
        beat = jnp.zeros((1, tm), I32)
        for h in range(N_EXPERT_GROUPS):
            if h == g:
                continue
            wins = (gscore[h] > gscore[g]) | ((gscore[h] == gscore[g]) & (h < g))
            beat = beat + wins.astype(I32)
        keep = beat < TOPK_GROUPS
        cg = choice[g * per:(g + 1) * per, :]
        masked_parts.append(jnp.where(keep, cg, neg))
    masked = jnp.concatenate(masked_parts, axis=0)

    eidx = lax.broadcasted_iota(I32, (n_exp, tm), 0)
    hot = jnp.zeros((n_exp, tm), F32)
    sel_idx, sel_w, sel_hot = [], [], []
    for _ in range(TOP_K):
        mk = jnp.max(masked, axis=0, keepdims=True)
        ik = jnp.min(jnp.where(masked == mk, eidx, n_exp), axis=0, keepdims=True)
        one = eidx == ik
        sel_idx.append(ik)
        sel_w.append(jnp.sum(jnp.where(one, scores, 0.0), axis=0, keepdims=True))
        sel_hot.append(one)
        hot = hot + one.astype(F32)
        masked = jnp.where(one, neg, masked)

    wsum = sel_w[0]
    for k in range(1, TOP_K):
        wsum = wsum + sel_w[k]

    r = lax.broadcasted_iota(I32, (tm, tm), 0)
    c = lax.broadcasted_iota(I32, (tm, tm), 1)
    upper = (r < c).astype(BF16)
    pref = jnp.dot(hot.astype(BF16), upper, preferred_element_type=F32) + carry_ref[...]
    for k in range(TOP_K):
        idx_ref[k:k + 1, :] = sel_idx[k]
        wt_ref[k:k + 1, :] = sel_w[k] / wsum * ROUTED_SCALE
        rk = jnp.sum(jnp.where(sel_hot[k], pref, 0.0), axis=0, keepdims=True)
        rank_ref[k:k + 1, :] = rk.astype(I32)
    carry_ref[...] = carry_ref[...] + jnp.sum(hot, axis=1, keepdims=True)
    cnt_ref[...] = carry_ref[...]


def _dest_kernel(idx_ref, rank_ref, off_ref, o_ref, *, tm, n_exp):
    eidx = lax.broadcasted_iota(I32, (n_exp, tm), 0)
    off = off_ref[...]
    for k in range(TOP_K):
        hit = eidx == idx_ref[k:k + 1, :]
        base = jnp.sum(jnp.where(hit, off, 0.0), axis=0, keepdims=True)
        o_ref[k:k + 1, :] = base.astype(I32) + rank_ref[k:k + 1, :]


def _dest_rows(eidx, rank, pad_off_col):
    K, T = eidx.shape
    n_exp = pad_off_col.shape[0]
    tm = _pick(T, (1024, 512, 256, 128))
    kern = functools.partial(_dest_kernel, tm=tm, n_exp=n_exp)
    return pl.pallas_call(
        kern,
        grid=(T // tm,),
        in_specs=[
            pl.BlockSpec((K, tm), lambda i: (0, i)),
            pl.BlockSpec((K, tm), lambda i: (0, i)),
            pl.BlockSpec((n_exp, 1), lambda i: (0, 0)),
        ],
        out_specs=pl.BlockSpec((K, tm), lambda i: (0, i)),
        out_shape=jax.ShapeDtypeStruct((K, T), I32),
        compiler_params=_cparams(("arbitrary",)),
        name="moe_dest_rows",
    )(eidx, rank, pad_off_col)


def _dispatch_kernel(cnt_ref, off_ref, nbu_ref, dest_ref, x_ref, xs_ref, zero_ref, sem, *,
                     tm, blk, n_exp, exp_per_step, n_blocks, tail_per_step):
    step = pl.program_id(0)

    def tile(ref, r):
        return ref.at[pl.ds(pl.multiple_of(r * ROW_SUB, ROW_SUB), ROW_SUB), :]

    def issue(t, carry):
        src = tile(x_ref, t)
        for k in range(TOP_K):
            pltpu.make_async_copy(src, tile(xs_ref, dest_ref[t * TOP_K + k]), sem).start(priority=k % 2)
        return carry

    lax.fori_loop(0, tm, issue, 0)

    zero_ref[...] = jnp.zeros(zero_ref.shape, U32)

    def zero_copy(d):
        return pltpu.make_async_copy(tile(zero_ref, 0), tile(xs_ref, d), sem)

    def zero_block(bi):
        rows = pl.ds(pl.multiple_of(bi * (blk * ROW_SUB), blk * ROW_SUB), blk * ROW_SUB)
        return pltpu.make_async_copy(zero_ref, xs_ref.at[rows, :], sem)

    tail_blocks = []
    for j in range(tail_per_step):
        bi = nbu_ref[0] + step * tail_per_step + j
        tail_blocks.append(bi)

        @pl.when(bi < n_blocks)
        def _():
            zero_block(bi).start()

    n_pad_total = jnp.int32(0)
    for j in range(exp_per_step):
        e = jnp.minimum(step * exp_per_step + j, n_exp - 1)
        valid = (step * exp_per_step + j) < n_exp
        cnt = cnt_ref[e]
        n_pad = jnp.where(valid, (blk - cnt % blk) % blk, 0)
        base = off_ref[e] + cnt

        def zissue(i, carry, base=base):
            zero_copy(base + i).start()
            return carry

        lax.fori_loop(0, n_pad, zissue, 0)
        n_pad_total = n_pad_total + n_pad

    for k in range(TOP_K):
        pltpu.make_async_copy(x_ref, xs_ref.at[pl.ds(0, tm * ROW_SUB), :], sem).wait()

    def wait(i, carry):
        zero_copy(0).wait()
        return carry

    lax.fori_loop(0, n_pad_total, wait, 0)

    for bi in tail_blocks:
        @pl.when(bi < n_blocks)
        def _():
            zero_block(0).wait()


def _dispatch(counts, pad_off, nb_used, dest_flat, x1p, n_rows_out, blk):
    T = x1p.shape[0] // ROW_SUB
    n_exp = counts.shape[0]
    tm = _pick(T, (512, 256, 128))
    n_steps = T // tm
    exp_per_step = -(-n_exp // n_steps)
    n_blocks = n_rows_out // blk
    max_tail = n_blocks - (T * TOP_K) // blk
    tail_per_step = -(-max_tail // n_steps)
    kern = functools.partial(_dispatch_kernel, tm=tm, blk=blk, n_exp=n_exp,
                             exp_per_step=exp_per_step, n_blocks=n_blocks,
                             tail_per_step=tail_per_step)
    grid_spec = pltpu.PrefetchScalarGridSpec(
        num_scalar_prefetch=3,
        grid=(n_steps,),
        in_specs=[
            pl.BlockSpec((tm * TOP_K,), lambda i, c, o, n: (i,), memory_space=pltpu.SMEM),
            pl.BlockSpec((tm * ROW_SUB, LANES), lambda i, c, o, n: (i, 0)),
        ],
        out_specs=pl.BlockSpec(memory_space=pl.ANY),
        scratch_shapes=[pltpu.VMEM((blk * ROW_SUB, LANES), U32), pltpu.SemaphoreType.DMA(())],
    )
    return pl.pallas_call(
        kern,
        grid_spec=grid_spec,
        out_shape=jax.ShapeDtypeStruct((n_rows_out * ROW_SUB, LANES), U32),
        compiler_params=_cparams(("arbitrary",)),
        name="moe_dispatch",
    )(counts, pad_off, nb_used, dest_flat, x1p)


def _gmm_kernel(be_ref, nb_ref, fresh_ref, slot_ref, next_ref,
                xs_ref, wg_ref, wu_ref, wd_ref, ys_ref,
                wg_f, wu_f, wd_f, wg_b, wu_b, wd_b, sem):
    b = pl.program_id(0)
    live = b < nb_ref[0]

    def fetch(e, s):
        return (pltpu.make_async_copy(wg_ref.at[e], wg_f.at[s], sem.at[s]),
                pltpu.make_async_copy(wu_ref.at[e], wu_f.at[s], sem.at[s]),
                pltpu.make_async_copy(wd_ref.at[e], wd_f.at[s], sem.at[s]))

    @pl.when(b == 0)
    def _():
        for c in fetch(be_ref[0], 0):
            c.start()

    @pl.when(live & (fresh_ref[b] == 1))
    def _():
        s = slot_ref[b]
        for c in fetch(be_ref[b], s):
            c.wait()
        nxt = next_ref[b]

        @pl.when(nxt >= 0)
        def _():
            for c in fetch(nxt, 1 - s):
                c.start()

        wg_b[...] = wg_f[s].astype(BF16)
        wu_b[...] = wu_f[s].astype(BF16)
        wd_b[...] = wd_f[s].astype(BF16)

    @pl.when(live)
    def _():
        blk = xs_ref.shape[0] // ROW_SUB
        lo, hi = _unpack_pairs(_load_tiles_as_rows(xs_ref, blk))
        lo = lo.astype(BF16)
        hi = hi.astype(BF16)
        half = lo.shape[1]
        g = (jnp.dot(lo, wg_b[:half, :], preferred_element_type=F32)
             + jnp.dot(hi, wg_b[half:, :], preferred_element_type=F32))
        u = (jnp.dot(lo, wu_b[:half, :], preferred_element_type=F32)
             + jnp.dot(hi, wu_b[half:, :], preferred_element_type=F32))
        h = (g * jax.nn.sigmoid(g) * u).astype(BF16)
        y_lo = jnp.dot(h, wd_b[:, :half], preferred_element_type=F32)
        y_hi = jnp.dot(h, wd_b[:, half:], preferred_element_type=F32)
        _store_rows_as_tiles(ys_ref, _pack_halves(y_lo, y_hi))

    @pl.when(jnp.logical_not(live))
    def _():
        ys_ref[...] = jnp.zeros(ys_ref.shape, U32)


def _gmm(block_expert, nb_used, fresh, slot, next_expert, xs, wg, wu, wd, blk):
    P = xs.shape[0] // ROW_SUB
    n_exp, D, FF = wg.shape
    nb = P // blk

    def row_map(b, be, nbu, fr, sl, nx):
        return (jnp.minimum(b, nbu[0] - 1), 0)

    grid_spec = pltpu.PrefetchScalarGridSpec(
        num_scalar_prefetch=5,
        grid=(nb,),
        in_specs=[
            pl.BlockSpec((blk * ROW_SUB, LANES), row_map),
            pl.BlockSpec(memory_space=pl.ANY),
            pl.BlockSpec(memory_space=pl.ANY),
            pl.BlockSpec(memory_space=pl.ANY),
        ],
        out_specs=pl.BlockSpec((blk * ROW_SUB, LANES), lambda b, be, nbu, fr, sl, nx: (b, 0)),
        scratch_shapes=[
            pltpu.VMEM((2, D, FF), F32), pltpu.VMEM((2, D, FF), F32), pltpu.VMEM((2, FF, D), F32),
            pltpu.VMEM((D, FF), BF16), pltpu.VMEM((D, FF), BF16), pltpu.VMEM((FF, D), BF16),
            pltpu.SemaphoreType.DMA((2,)),
        ],
    )
    return pl.pallas_call(
        _gmm_kernel,
        grid_spec=grid_spec,
        out_shape=jax.ShapeDtypeStruct((P * ROW_SUB, LANES), U32),
        compiler_params=_cparams(("arbitrary",)),
        name="moe_grouped_swiglu",
    )(block_expert, nb_used, fresh, slot, next_expert, xs, wg, wu, wd)


def _combine_kernel(dest_ref, wt_ref, x_ref, ys_ref, wg_ref, wu_ref, wd_ref, g_ref, b_ref,
                    o_ref, buf, sem, *, tm):
    def tile(ref, r):
        return ref.at[pl.ds(pl.multiple_of(r * ROW_SUB, ROW_SUB), ROW_SUB), :]

    def issue(t, carry):
        for k in range(TOP_K):
            pltpu.make_async_copy(tile(ys_ref, dest_ref[t * TOP_K + k]), tile(buf.at[k], t),
                                  sem).start(priority=k % 2)
        return carry

    lax.fori_loop(0, tm, issue, 0)

    x = x_ref[...]
    xb = x.astype(BF16)
    g = jnp.dot(xb, wg_ref[...], preferred_element_type=F32)
    u = jnp.dot(xb, wu_ref[...], preferred_element_type=F32)
    h = (g * jax.nn.sigmoid(g) * u).astype(BF16)
    shared = jnp.dot(h, wd_ref[...], preferred_element_type=F32)

    for k in range(TOP_K):
        pltpu.make_async_copy(ys_ref.at[pl.ds(0, tm * ROW_SUB), :], buf.at[k], sem).wait()

    half = x.shape[1] // 2
    wt = wt_ref[...]
    r_lo = jnp.zeros((tm, half), F32)
    r_hi = jnp.zeros((tm, half), F32)
    for k in range(TOP_K):
        lo, hi = _unpack_pairs(_load_tiles_as_rows(buf.at[k], tm))
        wk = wt[:, k:k + 1]
        r_lo = r_lo + lo * wk
        r_hi = r_hi + hi * wk
    routed = jnp.concatenate([r_lo, r_hi], axis=1)
    y = ALPHA * x + (routed + shared)
    o_ref[...] = _layer_norm(y, g_ref[...], b_ref[...])


def _combine(dest_flat, wt_tok, x1, ys, wg_s, wu_s, wd_s, g, b):
    T, D = x1.shape
    FF = wg_s.shape[1]
    tm = _pick(T, (256, 128))
    n_steps = T // tm
    kern = functools.partial(_combine_kernel, tm=tm)
    return pl.pallas_call(
        kern,
        grid=(n_steps,),
        in_specs=[
            pl.BlockSpec((tm * TOP_K,), lambda i: (i,), memory_space=pltpu.SMEM),
            pl.BlockSpec((tm, TOP_K), lambda i: (i, 0)),
            pl.BlockSpec((tm, D), lambda i: (i, 0)),
            pl.BlockSpec(memory_space=pl.ANY),
            pl.BlockSpec((D, FF), lambda i: (0, 0)),
            pl.BlockSpec((D, FF), lambda i: (0, 0)),
            pl.BlockSpec((FF, D), lambda i: (0, 0)),
            pl.BlockSpec((1, D), lambda i: (0, 0)),
            pl.BlockSpec((1, D), lambda i: (0, 0)),
        ],
        out_specs=pl.BlockSpec((tm, D), lambda i: (i, 0)),
        out_shape=jax.ShapeDtypeStruct((T, D), F32),
        scratch_shapes=[pltpu.VMEM((TOP_K, tm * ROW_SUB, LANES), U32), pltpu.SemaphoreType.DMA(())],
        compiler_params=_cparams(("arbitrary",)),
        name="moe_combine_ln2",
    )(dest_flat, wt_tok, x1, ys, wg_s, wu_s, wd_s, g, b)


def _rope_tables(positions):
    half = ROT_DIM // 2
    inv_freq = ROPE_THETA ** (-jnp.arange(0, ROT_DIM, 2, dtype=F32) / ROT_DIM)
    ang = positions.reshape(-1).astype(F32)[:, None] * inv_freq
    cos = jnp.cos(ang)
    sin = jnp.sin(ang)
    T = ang.shape[0]
    pad = HEAD_DIM - ROT_DIM
    c64 = jnp.concatenate([cos, cos, jnp.ones((T, pad), F32)], axis=1)
    sa64 = jnp.concatenate([jnp.zeros((T, half), F32), sin, jnp.zeros((T, pad), F32)], axis=1)
    sb64 = jnp.concatenate([-sin, jnp.zeros((T, half + pad), F32)], axis=1)
    rep = LANES // HEAD_DIM
    return jnp.tile(c64, (1, rep)), jnp.tile(sa64, (1, rep)), jnp.tile(sb64, (1, rep))


def kernel(x, positions, w_in, lam_q1, lam_k1, lam_q2, lam_k2, subln_g, sgu_ln_g, sgu_ln_b,
           w_spatial, b_spatial, w_out, ln1_g, ln1_b, w_router, router_bias, w_gate_exp,
           w_up_exp, w_down_exp, w_gate_sh, w_up_sh, w_down_sh, ln2_g, ln2_b):
    B, S, D = x.shape
    T = B * S
    l = 0
    x2 = x.reshape(T, D)
    n_exp = w_router.shape[-1]
    blk = MOE_BLOCK_ROWS

    w_in_b = w_in[l].astype(BF16)
    w_out_b = w_out[l].astype(BF16)
    rc, rsa, rsb = _rope_tables(positions)
    lam = (jnp.exp(jnp.sum(lam_q1[l].astype(F32) * lam_k1[l].astype(F32)))
           - jnp.exp(jnp.sum(lam_q2[l].astype(F32) * lam_k2[l].astype(F32)))
           + LAMBDA_INIT).reshape(1, 1).astype(F32)
    bs_full = jnp.broadcast_to(b_spatial[l][:, :, None], (N_SG_GROUPS, SG_CHUNK, SG_DIM)).astype(F32)
    w_rt = w_router[l].astype(F32).T
    w_rt_hi = w_rt.astype(BF16)
    w_rt_lo = (w_rt - w_rt_hi.astype(F32)).astype(BF16)
    bias_col = router_bias[l].astype(F32).reshape(n_exp, 1)

    proj = _inproj(x2, w_in_b, rc, rsa, rsb)
    attn = _attention(proj, lam, subln_g[l].reshape(1, V_DIM).astype(F32), B, S)
    sg = _sgu(proj, sgu_ln_g[l].astype(F32), sgu_ln_b[l].astype(F32), w_spatial[l], bs_full)
    x1, x1p, eidx, wt, rank, cnt = _outproj_router(
        attn, sg, x2, w_out_b, ln1_g[l].reshape(1, D), ln1_b[l].reshape(1, D),
        w_rt_hi, w_rt_lo, bias_col)

    counts = cnt.reshape(n_exp).astype(I32)
    padded = ((counts + blk - 1) // blk) * blk
    pad_end = jnp.cumsum(padded)
    pad_off = pad_end - padded
    P = T * TOP_K + n_exp * blk
    assert P < 2 ** 24
    dest = _dest_rows(eidx, rank, pad_off.astype(F32).reshape(n_exp, 1))
    dest_flat = dest.T.reshape(T * TOP_K)
    wt_tok = wt.T
    nb = P // blk
    block_start = jnp.arange(nb, dtype=I32) * blk
    block_expert = jnp.minimum(
        jnp.sum((pad_end[None, :] <= block_start[:, None]).astype(I32), axis=1), n_exp - 1)
    nb_used = (pad_end[-1] // blk).reshape(1).astype(I32)
    live_blk = jnp.arange(nb, dtype=I32) < nb_used[0]
    fresh = jnp.concatenate([jnp.ones((1,), I32),
                             (block_expert[1:] != block_expert[:-1]).astype(I32)])
    slot = ((jnp.cumsum(fresh) - 1) % 2).astype(I32)
    later = live_blk[None, :] & (block_expert[None, :] > block_expert[:, None])
    next_expert = jnp.min(jnp.where(later, block_expert[None, :], n_exp), axis=1)
    next_expert = jnp.where(next_expert < n_exp, next_expert, -1).astype(I32)

    xs = _dispatch(counts, pad_off.astype(I32), nb_used, dest_flat, x1p, P, blk)
    ys = _gmm(block_expert, nb_used, fresh, slot, next_expert, xs,
              w_gate_exp[l], w_up_exp[l], w_down_exp[l], blk)
    out = _combine(dest_flat, wt_tok, x1, ys,
                   w_gate_sh[l].astype(BF16), w_up_sh[l].astype(BF16), w_down_sh[l].astype(BF16),
                   ln2_g[l].reshape(1, D), ln2_b[l].reshape(1, D))
    return out.reshape(B, S, D)
```

```python
import functools
import math

import jax
import jax.numpy as jnp
from jax import lax
from jax.experimental import pallas as pl
from jax.experimental.pallas import tpu as pltpu

F32 = jnp.float32
BF16 = jnp.bfloat16
U32 = jnp.uint32
I32 = jnp.int32

N_HEADS = 8
HEAD_DIM = 64
V_DIM = 2 * HEAD_DIM
N_SG_GROUPS = 8
SG_DIM = 128
SG_CHUNK = 128
ROPE_THETA = 500000.0
ROT_DIM = HEAD_DIM // 4
TOP_K = 8
N_EXPERT_GROUPS = 8
TOPK_GROUPS = 4
ROUTED_SCALE = 2.5
LN_EPS = 1e-5
DEPTH = 1
ALPHA = (2 * DEPTH) ** 0.25
LAMBDA_INIT = 0.8 - 0.6 * math.exp(-0.3 * 0)
QK_SCALE = HEAD_DIM ** -0.5 * math.log2(math.e)

LANES = 128
SEG = 1024
ATTN_TQ = (512, 256, 128)
ATTN_TK = (1024, 512, 256, 128)
MOE_BLOCK_ROWS = 256
VMEM_LIMIT = 56 * 1024 * 1024


def _cparams(sem, vmem=VMEM_LIMIT):
    return pltpu.CompilerParams(dimension_semantics=sem, vmem_limit_bytes=vmem)


def _pick(n, prefs):
    for p in prefs:
        if n % p == 0:
            return p
    return n


def _inproj_kernel(x_ref, w_ref, c_ref, sa_ref, sb_ref, o_ref, xb_ref):
    j = pl.program_id(1)

    @pl.when(j == 0)
    def _():
        xb_ref[...] = x_ref[...].astype(BF16)

    acc = jnp.dot(xb_ref[...], w_ref[...], preferred_element_type=F32)

    @pl.when(j < 2)
    def _():
        scale = jnp.where(j == 0, QK_SCALE, 1.0).astype(F32)
        c = c_ref[...] * scale
        sa = sa_ref[...] * scale
        sb = sb_ref[...] * scale
        for blk in range(SEG // LANES):
            a = acc[:, blk * LANES:(blk + 1) * LANES]
            r = (a * c + pltpu.roll(a, ROT_DIM // 2, 1) * sa
                 + pltpu.roll(a, LANES - ROT_DIM // 2, 1) * sb)
            o_ref[:, blk * LANES:(blk + 1) * LANES] = r.astype(BF16)

    @pl.when(j == 2)
    def _():
        o_ref[...] = acc.astype(BF16)

    @pl.when(j > 2)
    def _():
        g = 0.5 * acc * (1.0 + lax.erf(acc * (2.0 ** -0.5)))
        o_ref[...] = g.astype(BF16)


def _inproj(x2, w_b, rc, rsa, rsb):
    T, D = x2.shape
    n_seg = w_b.shape[1] // SEG
    tm = _pick(T, (1024, 512, 256, 128))
    return pl.pallas_call(
        _inproj_kernel,
        grid=(T // tm, n_seg),
        in_specs=[
            pl.BlockSpec((tm, D), lambda i, j: (i, 0)),
            pl.BlockSpec((D, SEG), lambda i, j: (0, j)),
            pl.BlockSpec((tm, LANES), lambda i, j: (i, 0)),
            pl.BlockSpec((tm, LANES), lambda i, j: (i, 0)),
            pl.BlockSpec((tm, LANES), lambda i, j: (i, 0)),
        ],
        out_specs=pl.BlockSpec((tm, SEG), lambda i, j: (i, j)),
        out_shape=jax.ShapeDtypeStruct((T, n_seg * SEG), BF16),
        scratch_shapes=[pltpu.VMEM((tm, D), BF16)],
        compiler_params=_cparams(("arbitrary", "arbitrary")),
        name="inproj",
    )(x2, w_b, rc, rsa, rsb)


def _attn_kernel(lam_ref, q_ref, k_ref, v_ref, g_ref, o_ref,
                 acc_a, acc_b, m_a, m_b, s_a0, s_a1, s_b0, s_b1, *, tq, tk):
    qi = pl.program_id(2)
    s_a = (s_a0, s_a1)
    s_b = (s_b0, s_b1)
    dn = (((1,), (1,)), ((), ()))
    q = q_ref[...]
    lane = lax.broadcasted_iota(I32, q.shape, 1)
    zero = jnp.zeros_like(q)
    q_a = jnp.where(lane < HEAD_DIM, q, zero)
    q_b = jnp.where(lane >= HEAD_DIM, q, zero)

    for r in (acc_a, acc_b):
        r[...] = jnp.zeros(r.shape, F32)
    for r in (m_a, m_b):
        r[...] = jnp.full(r.shape, -jnp.inf, F32)

    def scores(ki, slot):
        start = pl.multiple_of(ki * tk, tk)
        k = k_ref[pl.ds(start, tk), :]
        s_a[slot][...] = lax.dot_general(q_a, k, dn, preferred_element_type=F32)
        s_b[slot][...] = lax.dot_general(q_b, k, dn, preferred_element_type=F32)


    def consume(ki, slot, masked):
        start = pl.multiple_of(ki * tk, tk)
        v = v_ref[pl.ds(start, tk), :]
        v_ext = jnp.concatenate([v, jnp.ones_like(v)], axis=1)
        if masked:
            row = lax.broadcasted_iota(I32, (tq, tk), 0) + (qi * tq - start)
            col = lax.broadcasted_iota(I32, (tq, tk), 1)
            keep = col <= row
        for s_ref, acc, m in ((s_a[slot], acc_a, m_a), (s_b[slot], acc_b, m_b)):
            s = s_ref[...]
            if masked:
                s = jnp.where(keep, s, -jnp.inf)
            m_prev = m[...]
            m_new = jnp.maximum(m_prev, jnp.max(s, axis=1, keepdims=True))
            alpha = jnp.exp2(m_prev - m_new)
            p = jnp.exp2(s - m_new)
            acc[...] = alpha * acc[...] + jnp.dot(p.astype(BF16), v_ext, preferred_element_type=F32)
            m[...] = m_new

    n_full = (qi * tq) // tk
    n_quads = n_full // 4
    n_pairs = n_full // 2
    scores(0, 0)

    def pair(ki):
        scores(ki + 1, 1)
        consume(ki, 0, False)
        scores(ki + 2, 0)
        consume(ki + 1, 1, False)

    def quad_body(t, carry):
        pair(4 * t)
        pair(4 * t + 2)
        return carry

    def pair_body(t, carry):
        pair(2 * t)
        return carry

    lax.fori_loop(0, n_quads, quad_body, 0)
    lax.fori_loop(2 * n_quads, n_pairs, pair_body, 0)

    @pl.when(n_full % 2 == 1)
    def _():
        scores(n_full, 1)
        consume(n_full - 1, 0, False)
        consume(n_full, 1, True)

    @pl.when(n_full % 2 == 0)
    def _():
        consume(n_full, 0, True)

    lam = lam_ref[0, 0]
    o = (acc_a[:, :V_DIM] / acc_a[:, V_DIM:]) - lam * (acc_b[:, :V_DIM] / acc_b[:, V_DIM:])
    ms = jnp.mean(o * o, axis=1, keepdims=True)
    o = o * lax.rsqrt(ms + LN_EPS) * g_ref[...] * (1.0 - LAMBDA_INIT)
    o_ref[...] = o.astype(BF16)


def _attention(proj, lam, subln_g, B, S):
    T = B * S
    tq = _pick(S, ATTN_TQ)
    nq = S // tq
    tk = _pick(S, ATTN_TK)
    assert tk % tq == 0
    kern = functools.partial(_attn_kernel, tq=tq, tk=tk)
    return pl.pallas_call(
        kern,
        grid=(B, N_HEADS, nq),
        in_specs=[
            pl.BlockSpec(memory_space=pltpu.SMEM),
            pl.BlockSpec((tq, V_DIM), lambda b, h, i: (b * nq + i, h)),
            pl.BlockSpec((S, V_DIM), lambda b, h, i: (b, N_HEADS + h)),
            pl.BlockSpec((S, V_DIM), lambda b, h, i: (b, 2 * N_HEADS + h)),
            pl.BlockSpec((1, V_DIM), lambda b, h, i: (0, 0)),
        ],
        out_specs=pl.BlockSpec((tq, V_DIM), lambda b, h, i: (b * nq + i, h)),
        out_shape=jax.ShapeDtypeStruct((T, N_HEADS * V_DIM), BF16),
        scratch_shapes=[
            pltpu.VMEM((tq, 2 * V_DIM), F32), pltpu.VMEM((tq, 2 * V_DIM), F32),
            pltpu.VMEM((tq, 1), F32), pltpu.VMEM((tq, 1), F32),
            pltpu.VMEM((tq, tk), F32), pltpu.VMEM((tq, tk), F32),
            pltpu.VMEM((tq, tk), F32), pltpu.VMEM((tq, tk), F32),
        ],
        compiler_params=_cparams(("arbitrary", "arbitrary", "arbitrary")),
        name="diff_attention",
    )(lam, proj, proj, proj, subln_g)


def _sgu_kernel(u_ref, v_ref, g_ref, b_ref, w_ref, bs_ref, o_ref, *, tm):
    g = pl.program_id(1)
    ln_g = g_ref[pl.ds(g, 1), :]
    ln_b = b_ref[pl.ds(g, 1), :]
    row = lax.broadcasted_iota(I32, (SG_CHUNK, SG_CHUNK), 0)
    col = lax.broadcasted_iota(I32, (SG_CHUNK, SG_CHUNK), 1)
    w = jnp.where(col <= row, w_ref[0], 0.0).astype(BF16)
    bs = bs_ref[0]
    for c in range(tm // SG_CHUNK):
        sl = slice(c * SG_CHUNK, (c + 1) * SG_CHUNK)
        v = v_ref[sl, :].astype(F32)
        mu = jnp.mean(v, axis=1, keepdims=True)
        vc = v - mu
        var = jnp.mean(vc * vc, axis=1, keepdims=True)
        vn = vc * lax.rsqrt(var + LN_EPS) * ln_g + ln_b
        mixed = jnp.dot(w, vn.astype(BF16), preferred_element_type=F32) + bs
        o_ref[sl, :] = (u_ref[sl, :].astype(F32) * mixed).astype(BF16)


def _sgu(proj, ln_g, ln_b, w_s, bs_full):
    T = proj.shape[0]
    tm = _pick(T, (2048, 1024, 512, 256, 128))
    u_blk = 3 * SEG // SG_DIM
    v_blk = 4 * SEG // SG_DIM
    kern = functools.partial(_sgu_kernel, tm=tm)
    return pl.pallas_call(
        kern,
        grid=(T // tm, N_SG_GROUPS),
        in_specs=[
            pl.BlockSpec((tm, SG_DIM), lambda i, g: (i, u_blk + g)),
            pl.BlockSpec((tm, SG_DIM), lambda i, g: (i, v_blk + g)),
            pl.BlockSpec((N_SG_GROUPS, SG_DIM), lambda i, g: (0, 0)),
            pl.BlockSpec((N_SG_GROUPS, SG_DIM), lambda i, g: (0, 0)),
            pl.BlockSpec((1, SG_CHUNK, SG_CHUNK), lambda i, g: (g, 0, 0)),
            pl.BlockSpec((1, SG_CHUNK, SG_DIM), lambda i, g: (g, 0, 0)),
        ],
        out_specs=pl.BlockSpec((tm, SG_DIM), lambda i, g: (i, g)),
        out_shape=jax.ShapeDtypeStruct((T, N_SG_GROUPS * SG_DIM), BF16),
        compiler_params=_cparams(("arbitrary", "arbitrary")),
        name="spatial_gating",
    )(proj, proj, ln_g, ln_b, w_s, bs_full)


def _pack_halves(y_lo, y_hi):
    lo = lax.bitcast_convert_type(y_lo.astype(BF16).astype(F32), U32)
    hi = lax.bitcast_convert_type(y_hi.astype(BF16).astype(F32), U32)
    return (lo >> 16) | (hi & jnp.uint32(0xFFFF0000))


def _pack_pairs(y):
    half = y.shape[1] // 2
    return _pack_halves(y[:, :half], y[:, half:])


def _unpack_pairs(p):
    lo = lax.bitcast_convert_type(p << 16, F32)
    hi = lax.bitcast_convert_type(p & jnp.uint32(0xFFFF0000), F32)
    return lo, hi


ROW_SUB = 8


def _store_rows_as_tiles(ref, packed):
    n = packed.shape[0]
    for s in range(ROW_SUB):
        ref[pl.ds(s, n, stride=ROW_SUB), :] = packed[:, s * LANES:(s + 1) * LANES]


def _load_tiles_as_rows(ref, n):
    return jnp.concatenate(
        [ref[pl.ds(s, n, stride=ROW_SUB), :] for s in range(ROW_SUB)], axis=1)


def _layer_norm(y, g, b):
    mu = jnp.mean(y, axis=1, keepdims=True)
    yc = y - mu
    var = jnp.mean(yc * yc, axis=1, keepdims=True)
    return yc * lax.rsqrt(var + LN_EPS) * g + b


def _outproj_router_kernel(a_ref, s_ref, x_ref, w_ref, g_ref, b_ref, wh_ref, wl_ref, bias_ref,
                           x1_ref, xp_ref, idx_ref, wt_ref, rank_ref, cnt_ref, carry_ref, *,
                           tm, n_exp):
    half = w_ref.shape[0] // 2
    mix = jnp.dot(a_ref[...], w_ref[:half, :], preferred_element_type=F32)
    mix = mix + jnp.dot(s_ref[...], w_ref[half:, :], preferred_element_type=F32)
    y = ALPHA * x_ref[...] + mix
    x1 = _layer_norm(y, g_ref[...], b_ref[...])
    x1_ref[...] = x1
    _store_rows_as_tiles(xp_ref, _pack_pairs(x1))
    _route_tile(x1, wh_ref, wl_ref, bias_ref, idx_ref, wt_ref, rank_ref, cnt_ref, carry_ref,
                tm=tm, n_exp=n_exp)


def _outproj_router(attn, sg, x2, w_b, g, b, wh, wl, bias_col):
    T, D = x2.shape
    assert D // 2 == ROW_SUB * LANES
    n_exp = wh.shape[0]
    tm = _pick(T, (512, 256, 128))
    wa = attn.shape[1]
    ws = sg.shape[1]
    kern = functools.partial(_outproj_router_kernel, tm=tm, n_exp=n_exp)
    return pl.pallas_call(
        kern,
        grid=(T // tm,),
        in_specs=[
            pl.BlockSpec((tm, wa), lambda i: (i, 0)),
            pl.BlockSpec((tm, ws), lambda i: (i, 0)),
            pl.BlockSpec((tm, D), lambda i: (i, 0)),
            pl.BlockSpec((wa + ws, D), lambda i: (0, 0)),
            pl.BlockSpec((1, D), lambda i: (0, 0)),
            pl.BlockSpec((1, D), lambda i: (0, 0)),
            pl.BlockSpec((n_exp, D), lambda i: (0, 0)),
            pl.BlockSpec((n_exp, D), lambda i: (0, 0)),
            pl.BlockSpec((n_exp, 1), lambda i: (0, 0)),
        ],
        out_specs=[
            pl.BlockSpec((tm, D), lambda i: (i, 0)),
            pl.BlockSpec((tm * ROW_SUB, LANES), lambda i: (i, 0)),
            pl.BlockSpec((TOP_K, tm), lambda i: (0, i)),
            pl.BlockSpec((TOP_K, tm), lambda i: (0, i)),
            pl.BlockSpec((TOP_K, tm), lambda i: (0, i)),
            pl.BlockSpec((n_exp, 1), lambda i: (0, 0)),
        ],
        out_shape=[
            jax.ShapeDtypeStruct((T, D), F32),
            jax.ShapeDtypeStruct((T * ROW_SUB, LANES), U32),
            jax.ShapeDtypeStruct((TOP_K, T), I32),
            jax.ShapeDtypeStruct((TOP_K, T), F32),
            jax.ShapeDtypeStruct((TOP_K, T), I32),
            jax.ShapeDtypeStruct((n_exp, 1), F32),
        ],
        scratch_shapes=[pltpu.VMEM((n_exp, 1), F32)],
        compiler_params=_cparams(("arbitrary",)),
        name="outproj_ln1_router",
    )(attn, sg, x2, w_b, g, b, wh, wl, bias_col)


def _route_tile(x, wh_ref, wl_ref, bias_ref, idx_ref, wt_ref, rank_ref, cnt_ref,
                carry_ref, *, tm, n_exp):
    step = pl.program_id(0)

    @pl.when(step == 0)
    def _():
        carry_ref[...] = jnp.zeros(carry_ref.shape, F32)

    xh = x.astype(BF16)
    xl = (x - xh.astype(F32)).astype(BF16)
    dn = (((1,), (1,)), ((), ()))
    wh = wh_ref[...]
    logits = (lax.dot_general(wh, xh, dn, preferred_element_type=F32)
              + lax.dot_general(wh, xl, dn, preferred_element_type=F32)
              + lax.dot_general(wl_ref[...], xh, dn, preferred_element_type=F32))
    scores = jax.nn.sigmoid(logits)
    choice = scores + bias_ref[...]

    per = n_exp // N_EXPERT_GROUPS
    neg = jnp.float32(-jnp.inf)
    sub = lax.broadcasted_iota(I32, (per, tm), 0)
    gscore = []
    for g in range(N_EXPERT_GROUPS):
        cg = choice[g * per:(g + 1) * per, :]
        m1 = jnp.max(cg, axis=0, keepdims=True)
        i1 = jnp.min(jnp.where(cg == m1, sub, per), axis=0, keepdims=True)
        m2 = jnp.max(jnp.where(sub == i1, neg, cg), axis=0, keepdims=True)
        gscore.append(m1 + m2)
    masked_parts = []
    for g in range(N_EXPERT_GROUPS):
        beat = jnp.zeros((1, tm), I32)
        for h in range(N_EXPERT_GROUPS):
            if h == g:
                continue
            wins = (gscore[h] > gscore[g]) | ((gscore[h] == gscore[g]) & (h < g))
            beat = beat + wins.astype(I32)
        keep = beat < TOPK_GROUPS
        cg = choice[g * per:(g + 1) * per, :]
        masked_parts.append(jnp.where(keep, cg, neg))
    masked = jnp.concatenate(masked_parts, axis=0)

    eidx = lax.broadcasted_iota(I32, (n_exp, tm), 0)
    hot = jnp.zeros((n_exp, tm), F32)
    sel_idx, sel_w, sel_hot = [], [], []
    for _ in range(TOP_K):
        mk = jnp.max(masked, axis=0, keepdims=True)
        ik = jnp.min(jnp.where(masked == mk, eidx, n_exp), axis=0, keepdims=True)
        one = eidx == ik
        sel_idx.append(ik)
        sel_w.append(jnp.sum(jnp.where(one, scores, 0.0), axis=0, keepdims=True))
        sel_hot.append(one)
        hot = hot + one.astype(F32)
        masked = jnp.where(one, neg, masked)

    wsum = sel_w[0]
    for k in range(1, TOP_K):
        wsum = wsum + sel_w[k]

    r = lax.broadcasted_iota(I32, (tm, tm), 0)
    c = lax.broadcasted_iota(I32, (tm, tm), 1)
    upper = (r < c).astype(BF16)
    pref = jnp.dot(hot.astype(BF16), upper, preferred_element_type=F32) + carry_ref[...]
    for k in range(TOP_K):
        idx_ref[k:k + 1, :] = sel_idx[k]
        wt_ref[k:k + 1, :] = sel_w[k] / wsum * ROUTED_SCALE
        rk = jnp.sum(jnp.where(sel_hot[k], pref, 0.0), axis=0, keepdims=True)
        rank_ref[k:k + 1, :] = rk.astype(I32)
    carry_ref[...] = carry_ref[...] + jnp.sum(hot, axis=1, keepdims=True)
    cnt_ref[...] = carry_ref[...]


def _dest_kernel(idx_ref, rank_ref, off_ref, o_ref, *, tm, n_exp):
    eidx = lax.broadcasted_iota(I32, (n_exp, tm), 0)
    off = off_ref[...]
    for k in range(TOP_K):
        hit = eidx == idx_ref[k:k + 1, :]
        base = jnp.sum(jnp.where(hit, off, 0.0), axis=0, keepdims=True)
        o_ref[k:k + 1, :] = base.astype(I32) + rank_ref[k:k + 1, :]


def _dest_rows(eidx, rank, pad_off_col):
    K, T = eidx.shape
    n_exp = pad_off_col.shape[0]
    tm = _pick(T, (1024, 512, 256, 128))
    kern = functools.partial(_dest_kernel, tm=tm, n_exp=n_exp)
    return pl.pallas_call(
        kern,
        grid=(T // tm,),
        in_specs=[
            pl.BlockSpec((K, tm), lambda i: (0, i)),
            pl.BlockSpec((K, tm), lambda i: (0, i)),
            pl.BlockSpec((n_exp, 1), lambda i: (0, 0)),
        ],
        out_specs=pl.BlockSpec((K, tm), lambda i: (0, i)),
        out_shape=jax.ShapeDtypeStruct((K, T), I32),
        compiler_params=_cparams(("arbitrary",)),
        name="moe_dest_rows",
    )(eidx, rank, pad_off_col)


def _dispatch_kernel(cnt_ref, off_ref, nbu_ref, dest_ref, x_ref, xs_ref, zero_ref, sem, *,
                     tm, blk, n_exp, exp_per_step, n_blocks, tail_per_step):
    step = pl.program_id(0)

    def tile(ref, r):
        return ref.at[pl.ds(pl.multiple_of(r * ROW_SUB, ROW_SUB), ROW_SUB), :]

    def issue(t, carry):
        src = tile(x_ref, t)
        for k in range(TOP_K):
            pltpu.make_async_copy(src, tile(xs_ref, dest_ref[t * TOP_K + k]), sem).start(priority=k % 2)
        return carry

    lax.fori_loop(0, tm, issue, 0)

    zero_ref[...] = jnp.zeros(zero_ref.shape, U32)

    def zero_copy(d):
        return pltpu.make_async_copy(tile(zero_ref, 0), tile(xs_ref, d), sem)

    def zero_block(bi):
        rows = pl.ds(pl.multiple_of(bi * (blk * ROW_SUB), blk * ROW_SUB), blk * ROW_SUB)
        return pltpu.make_async_copy(zero_ref, xs_ref.at[rows, :], sem)

    tail_blocks = []
    for j in range(tail_per_step):
        bi = nbu_ref[0] + step * tail_per_step + j
        tail_blocks.append(bi)

        @pl.when(bi < n_blocks)
        def _():
            zero_block(bi).start()

    n_pad_total = jnp.int32(0)
    for j in range(exp_per_step):
        e = jnp.minimum(step * exp_per_step + j, n_exp - 1)
        valid = (step * exp_per_step + j) < n_exp
        cnt = cnt_ref[e]
        n_pad = jnp.where(valid, (blk - cnt % blk) % blk, 0)
        base = off_ref[e] + cnt

        def zissue(i, carry, base=base):
            zero_copy(base + i).start()
            return carry

        lax.fori_loop(0, n_pad, zissue, 0)
        n_pad_total = n_pad_total + n_pad

    for k in range(TOP_K):
        pltpu.make_async_copy(x_ref, xs_ref.at[pl.ds(0, tm * ROW_SUB), :], sem).wait()

    def wait(i, carry):
        zero_copy(0).wait()
        return carry

    lax.fori_loop(0, n_pad_total, wait, 0)

    for bi in tail_blocks:
        @pl.when(bi < n_blocks)
        def _():
            zero_block(0).wait()


def _dispatch(counts, pad_off, nb_used, dest_flat, x1p, n_rows_out, blk):
    T = x1p.shape[0] // ROW_SUB
    n_exp = counts.shape[0]
    tm = _pick(T, (512, 256, 128))
    n_steps = T // tm
    exp_per_step = -(-n_exp // n_steps)
    n_blocks = n_rows_out // blk
    max_tail = n_blocks - (T * TOP_K) // blk
    tail_per_step = -(-max_tail // n_steps)
    kern = functools.partial(_dispatch_kernel, tm=tm, blk=blk, n_exp=n_exp,
                             exp_per_step=exp_per_step, n_blocks=n_blocks,
                             tail_per_step=tail_per_step)
    grid_spec = pltpu.PrefetchScalarGridSpec(
        num_scalar_prefetch=3,
        grid=(n_steps,),
        in_specs=[
            pl.BlockSpec((tm * TOP_K,), lambda i, c, o, n: (i,), memory_space=pltpu.SMEM),
            pl.BlockSpec((tm * ROW_SUB, LANES), lambda i, c, o, n: (i, 0)),
        ],
        out_specs=pl.BlockSpec(memory_space=pl.ANY),
        scratch_shapes=[pltpu.VMEM((blk * ROW_SUB, LANES), U32), pltpu.SemaphoreType.DMA(())],
    )
    return pl.pallas_call(
        kern,
        grid_spec=grid_spec,
        out_shape=jax.ShapeDtypeStruct((n_rows_out * ROW_SUB, LANES), U32),
        compiler_params=_cparams(("arbitrary",)),
        name="moe_dispatch",
    )(counts, pad_off, nb_used, dest_flat, x1p)


def _gmm_kernel(be_ref, nb_ref, fresh_ref, slot_ref, next_ref,
                xs_ref, wg_ref, wu_ref, wd_ref, ys_ref,
                wg_f, wu_f, wd_f, wg_b, wu_b, wd_b, sem):
    b = pl.program_id(0)
    live = b < nb_ref[0]

    def fetch(e, s):
        return (pltpu.make_async_copy(wg_ref.at[e], wg_f.at[s], sem.at[s]),
                pltpu.make_async_copy(wu_ref.at[e], wu_f.at[s], sem.at[s]),
                pltpu.make_async_copy(wd_ref.at[e], wd_f.at[s], sem.at[s]))

    @pl.when(b == 0)
    def _():
        for c in fetch(be_ref[0], 0):
            c.start()

    @pl.when(live & (fresh_ref[b] == 1))
    def _():
        s = slot_ref[b]
        for c in fetch(be_ref[b], s):
            c.wait()
        nxt = next_ref[b]

        @pl.when(nxt >= 0)
        def _():
            for c in fetch(nxt, 1 - s):
                c.start()

        wg_b[...] = wg_f[s].astype(BF16)
        wu_b[...] = wu_f[s].astype(BF16)
        wd_b[...] = wd_f[s].astype(BF16)

    @pl.when(live)
    def _():
        blk = xs_ref.shape[0] // ROW_SUB
        lo, hi = _unpack_pairs(_load_tiles_as_rows(xs_ref, blk))
        lo = lo.astype(BF16)
        hi = hi.astype(BF16)
        half = lo.shape[1]
        g = (jnp.dot(lo, wg_b[:half, :], preferred_element_type=F32)
             + jnp.dot(hi, wg_b[half:, :], preferred_element_type=F32))
        u = (jnp.dot(lo, wu_b[:half, :], preferred_element_type=F32)
             + jnp.dot(hi, wu_b[half:, :], preferred_element_type=F32))
        h = (g * jax.nn.sigmoid(g) * u).astype(BF16)
        y_lo = jnp.dot(h, wd_b[:, :half], preferred_element_type=F32)
        y_hi = jnp.dot(h, wd_b[:, half:], preferred_element_type=F32)
        _store_rows_as_tiles(ys_ref, _pack_halves(y_lo, y_hi))

    @pl.when(jnp.logical_not(live))
    def _():
        ys_ref[...] = jnp.zeros(ys_ref.shape, U32)


def _gmm(block_expert, nb_used, fresh, slot, next_expert, xs, wg, wu, wd, blk):
    P = xs.shape[0] // ROW_SUB
    n_exp, D, FF = wg.shape
    nb = P // blk

    def row_map(b, be, nbu, fr, sl, nx):
        return (jnp.minimum(b, nbu[0] - 1), 0)

    grid_spec = pltpu.PrefetchScalarGridSpec(
        num_scalar_prefetch=5,
        grid=(nb,),
        in_specs=[
            pl.BlockSpec((blk * ROW_SUB, LANES), row_map),
            pl.BlockSpec(memory_space=pl.ANY),
            pl.BlockSpec(memory_space=pl.ANY),
            pl.BlockSpec(memory_space=pl.ANY),
        ],
        out_specs=pl.BlockSpec((blk * ROW_SUB, LANES), lambda b, be, nbu, fr, sl, nx: (b, 0)),
        scratch_shapes=[
            pltpu.VMEM((2, D, FF), F32), pltpu.VMEM((2, D, FF), F32), pltpu.VMEM((2, FF, D), F32),
            pltpu.VMEM((D, FF), BF16), pltpu.VMEM((D, FF), BF16), pltpu.VMEM((FF, D), BF16),
            pltpu.SemaphoreType.DMA((2,)),
        ],
    )
    return pl.pallas_call(
        _gmm_kernel,
        grid_spec=grid_spec,
        out_shape=jax.ShapeDtypeStruct((P * ROW_SUB, LANES), U32),
        compiler_params=_cparams(("arbitrary",)),
        name="moe_grouped_swiglu",
    )(block_expert, nb_used, fresh, slot, next_expert, xs, wg, wu, wd)


def _combine_kernel(dest_ref, wt_ref, x_ref, ys_ref, wg_ref, wu_ref, wd_ref, g_ref, b_ref,
                    o_ref, buf, sem, *, tm):
    def tile(ref, r):
        return ref.at[pl.ds(pl.multiple_of(r * ROW_SUB, ROW_SUB), ROW_SUB), :]

    def issue(t, carry):
        for k in range(TOP_K):
            pltpu.make_async_copy(tile(ys_ref, dest_ref[t * TOP_K + k]), tile(buf.at[k], t),
                                  sem).start(priority=k % 2)
        return carry

    lax.fori_loop(0, tm, issue, 0)

    x = x_ref[...]
    xb = x.astype(BF16)
    g = jnp.dot(xb, wg_ref[...], preferred_element_type=F32)
    u = jnp.dot(xb, wu_ref[...], preferred_element_type=F32)
    h = (g * jax.nn.sigmoid(g) * u).astype(BF16)
    shared = jnp.dot(h, wd_ref[...], preferred_element_type=F32)

    for k in range(TOP_K):
        pltpu.make_async_copy(ys_ref.at[pl.ds(0, tm * ROW_SUB), :], buf.at[k], sem).wait()

    half = x.shape[1] // 2
    wt = wt_ref[...]
    r_lo = jnp.zeros((tm, half), F32)
    r_hi = jnp.zeros((tm, half), F32)
    for k in range(TOP_K):
        lo, hi = _unpack_pairs(_load_tiles_as_rows(buf.at[k], tm))
        wk = wt[:, k:k + 1]
        r_lo = r_lo + lo * wk
        r_hi = r_hi + hi * wk
    routed = jnp.concatenate([r_lo, r_hi], axis=1)
    y = ALPHA * x + (routed + shared)
    o_ref[...] = _layer_norm(y, g_ref[...], b_ref[...])


def _combine(dest_flat, wt_tok, x1, ys, wg_s, wu_s, wd_s, g, b):
    T, D = x1.shape
    FF = wg_s.shape[1]
    tm = _pick(T, (256, 128))
    n_steps = T // tm
    kern = functools.partial(_combine_kernel, tm=tm)
    return pl.pallas_call(
        kern,
        grid=(n_steps,),
        in_specs=[
            pl.BlockSpec((tm * TOP_K,), lambda i: (i,), memory_space=pltpu.SMEM),
            pl.BlockSpec((tm, TOP_K), lambda i: (i, 0)),
            pl.BlockSpec((tm, D), lambda i: (i, 0)),
            pl.BlockSpec(memory_space=pl.ANY),
            pl.BlockSpec((D, FF), lambda i: (0, 0)),
            pl.BlockSpec((D, FF), lambda i: (0, 0)),
            pl.BlockSpec((FF, D), lambda i: (0, 0)),
            pl.BlockSpec((1, D), lambda i: (0, 0)),
            pl.BlockSpec((1, D), lambda i: (0, 0)),
        ],
        out_specs=pl.BlockSpec((tm, D), lambda i: (i, 0)),
        out_shape=jax.ShapeDtypeStruct((T, D), F32),
        scratch_shapes=[pltpu.VMEM((TOP_K, tm * ROW_SUB, LANES), U32), pltpu.SemaphoreType.DMA(())],
        compiler_params=_cparams(("arbitrary",)),
        name="moe_combine_ln2",
    )(dest_flat, wt_tok, x1, ys, wg_s, wu_s, wd_s, g, b)


def _rope_tables(positions):
    half = ROT_DIM // 2
    inv_freq = ROPE_THETA ** (-jnp.arange(0, ROT_DIM, 2, dtype=F32) / ROT_DIM)
    ang = positions.reshape(-1).astype(F32)[:, None] * inv_freq
    cos = jnp.cos(ang)
    sin = jnp.sin(ang)
    T = ang.shape[0]
    pad = HEAD_DIM - ROT_DIM
    c64 = jnp.concatenate([cos, cos, jnp.ones((T, pad), F32)], axis=1)
    sa64 = jnp.concatenate([jnp.zeros((T, half), F32), sin, jnp.zeros((T, pad), F32)], axis=1)
    sb64 = jnp.concatenate([-sin, jnp.zeros((T, half + pad), F32)], axis=1)
    rep = LANES // HEAD_DIM
    return jnp.tile(c64, (1, rep)), jnp.tile(sa64, (1, rep)), jnp.tile(sb64, (1, rep))


def kernel(x, positions, w_in, lam_q1, lam_k1, lam_q2, lam_k2, subln_g, sgu_ln_g, sgu_ln_b,
           w_spatial, b_spatial, w_out, ln1_g, ln1_b, w_router, router_bias, w_gate_exp,
           w_up_exp, w_down_exp, w_gate_sh, w_up_sh, w_down_sh, ln2_g, ln2_b):
    B, S, D = x.shape
    T = B * S
    l = 0
    x2 = x.reshape(T, D)
    n_exp = w_router.shape[-1]
    blk = MOE_BLOCK_ROWS

    w_in_b = w_in[l].astype(BF16)
    w_out_b = w_out[l].astype(BF16)
    rc, rsa, rsb = _rope_tables(positions)
    lam = (jnp.exp(jnp.sum(lam_q1[l].astype(F32) * lam_k1[l].astype(F32)))
           - jnp.exp(jnp.sum(lam_q2[l].astype(F32) * lam_k2[l].astype(F32)))
           + LAMBDA_INIT).reshape(1, 1).astype(F32)
    bs_full = jnp.broadcast_to(b_spatial[l][:, :, None], (N_SG_GROUPS, SG_CHUNK, SG_DIM)).astype(F32)
    w_rt = w_router[l].astype(F32).T
    w_rt_hi = w_rt.astype(BF16)
    w_rt_lo = (w_rt - w_rt_hi.astype(F32)).astype(BF16)
    bias_col = router_bias[l].astype(F32).reshape(n_exp, 1)

    proj = _inproj(x2, w_in_b, rc, rsa, rsb)
    attn = _attention(proj, lam, subln_g[l].reshape(1, V_DIM).astype(F32), B, S)
    sg = _sgu(proj, sgu_ln_g[l].astype(F32), sgu_ln_b[l].astype(F32), w_spatial[l], bs_full)
    x1, x1p, eidx, wt, rank, cnt = _outproj_router(
        attn, sg, x2, w_out_b, ln1_g[l].reshape(1, D), ln1_b[l].reshape(1, D),
        w_rt_hi, w_rt_lo, bias_col)

    counts = cnt.reshape(n_exp).astype(I32)
    padded = ((counts + blk - 1) // blk) * blk
    pad_end = jnp.cumsum(padded)
    pad_off = pad_end - padded
    P = T * TOP_K + n_exp * blk
    assert P < 2 ** 24
    dest = _dest_rows(eidx, rank, pad_off.astype(F32).reshape(n_exp, 1))
    dest_flat = dest.T.reshape(T * TOP_K)
    wt_tok = wt.T
    nb = P // blk
    block_start = jnp.arange(nb, dtype=I32) * blk
    block_expert = jnp.minimum(
        jnp.sum((pad_end[None, :] <= block_start[:, None]).astype(I32), axis=1), n_exp - 1)
    nb_used = (pad_end[-1] // blk).reshape(1).astype(I32)
    live_blk = jnp.arange(nb, dtype=I32) < nb_used[0]
    fresh = jnp.concatenate([jnp.ones((1,), I32),
                             (block_expert[1:] != block_expert[:-1]).astype(I32)])
    slot = ((jnp.cumsum(fresh) - 1) % 2).astype(I32)
    later = live_blk[None, :] & (block_expert[None, :] > block_expert[:, None])
    next_expert = jnp.min(jnp.where(later, block_expert[None, :], n_exp), axis=1)
    next_expert = jnp.where(next_expert < n_exp, next_expert, -1).astype(I32)

    xs = _dispatch(counts, pad_off.astype(I32), nb_used, dest_flat, x1p, P, blk)
    ys = _gmm(block_expert, nb_used, fresh, slot, next_expert, xs,
              w_gate_exp[l], w_up_exp[l], w_down_exp[l], blk)
    out = _combine(dest_flat, wt_tok, x1, ys,
                   w_gate_sh[l].astype(BF16), w_up_sh[l].astype(BF16), w_down_sh[l].astype(BF16),
                   ln2_g[l].reshape(1, D), ln2_b[l].reshape(1, D))
    return out.reshape(B, S, D)
```
